```python
import jax, jax.numpy as jnp
from jax import lax
import numpy as np

D_MODEL = 1024
BATCH = 4
SEQ = 4096
DEPTH = 4
DEC_BATCH = 32
DEC_SEQ = 16
PAST_LEN = 1024

CHUNK = 64
N_EVEN = (DEPTH + 1) // 2
N_ODD = DEPTH // 2
LRU_WIDTH = D_MODEL // 2
LRU_HEADS = 8
LRU_HD = LRU_WIDTH // LRU_HEADS
CONV_W = 4
LRU_C = 8.0
SB_HD = 64
SB_HEADS = (D_MODEL // 2) // SB_HD
SB_WIDTH = SB_HEADS * SB_HD
Q_BLOCK = 128
MIX_WIDTH = LRU_WIDTH + SB_WIDTH
IN_COLS = 2 * LRU_WIDTH + 3 * SB_WIDTH
POOL_WINDOWS = (2, 4, 8, 16)
POOL_GROUPS = len(POOL_WINDOWS)
POOL_GD = D_MODEL // POOL_GROUPS
POOL_BUF = max(POOL_WINDOWS) - 1
D_FF = 256 * ((8 * D_MODEL // 3 + 255) // 256)
N_EXPERTS = 8
TOP_K = 2
D_FF_EXPERT = D_FF
EPS = 1e-6

kernel_name = 'hybrid_rglru_stickbreak_pool_stream_step'


def rms_norm(x, g):
    xf = x.astype(jnp.float32)
    y = xf * lax.rsqrt(jnp.mean(xf * xf, axis=-1, keepdims=True) + EPS)
    return (y * g.astype(jnp.float32)).astype(x.dtype)


def causal_conv(x, buf, w, b):
    T = x.shape[1]
    xp = jnp.concatenate([buf, x], axis=1)
    y = b + xp[:, 0:T] * w[0]
    for j in range(1, CONV_W):
        y = y + xp[:, j:j + T] * w[j]
    return y, xp[:, xp.shape[1] - (CONV_W - 1):]


def _linear_combine(left, right):
    a_l, b_l = left
    a_r, b_r = right
    return a_l * a_r, a_r * b_l + b_r


def rg_lru(x, h0, wa, ba, wx, bx, lam):
    B, T, C = x.shape
    xh = x.reshape(B, T, LRU_HEADS, LRU_HD)
    r = jax.nn.sigmoid(jnp.einsum('bthi,hij->bthj', xh, wa).reshape(B, T, C) + ba)
    i = jax.nn.sigmoid(jnp.einsum('bthi,hij->bthj', xh, wx).reshape(B, T, C) + bx)
    log_a = -LRU_C * r * jax.nn.softplus(-lam)
    a = jnp.exp(log_a)
    u = jnp.sqrt(-jnp.expm1(2.0 * log_a)) * (i * x)
    u = u.at[:, 0].add(a[:, 0] * h0)
    _, h = lax.associative_scan(_linear_combine, (a, u), axis=1)
    return h, h[:, T - 1]


def sb_block(q, k, v):
    Tq, Tk = q.shape[1], k.shape[1]
    z = jnp.einsum('bqhd,bkhd->bhqk', q, k, preferred_element_type=jnp.float32) * (SB_HD ** -0.5)
    qpos = (Tk - Tq) + jnp.arange(Tq)
    valid = jnp.arange(Tk)[None, :] < qpos[:, None]
    log_beta = jax.nn.log_sigmoid(z)
    log_om = jnp.where(valid, log_beta - z, 0.0)
    shifted = jnp.concatenate([log_om[..., 1:], jnp.zeros_like(log_om[..., :1])], axis=-1)
    log_stick = lax.cumsum(shifted, axis=3, reverse=True)
    A = jnp.where(valid, jnp.exp(log_beta + log_stick), 0.0)
    return jnp.einsum('bhqk,bkhd->bqhd', A.astype(v.dtype), v)


def sb_attention(q, k_all, v_all):
    Tq, Tk = q.shape[1], k_all.shape[1]
    q_pos0 = Tk - Tq
    outs = []
    for qs in range(0, Tq, Q_BLOCK):
        qe = min(qs + Q_BLOCK, Tq)
        ke = q_pos0 + qe
        outs.append(sb_block(q[:, qs:qe], k_all[:, :ke], v_all[:, :ke]))
    return jnp.concatenate(outs, axis=1)


def mixer_ab(h, conv_buf, h0, kc, vc, w_in, w_out, conv_w, conv_b, wa, ba, wx, bx, lam):
    B, T, _ = h.shape
    proj = h @ w_in
    L, S = LRU_WIDTH, SB_WIDTH
    gate_br, lru_in, q, k, v = jnp.split(proj, [L, 2 * L, 2 * L + S, 2 * L + 2 * S], axis=-1)
    xc, new_buf = causal_conv(lru_in, conv_buf, conv_w, conv_b)
    hs, h_last = rg_lru(xc, h0, wa, ba, wx, bx, lam)
    y_lru = hs * jax.nn.gelu(gate_br)
    q = q.reshape(B, T, SB_HEADS, SB_HD)
    k = k.reshape(B, T, SB_HEADS, SB_HD)
    v = v.reshape(B, T, SB_HEADS, SB_HD)
    k_all = k if kc is None else jnp.concatenate([kc, k], axis=1)
    v_all = v if vc is None else jnp.concatenate([vc, v], axis=1)
    o = sb_attention(q, k_all, v_all).reshape(B, T, SB_WIDTH)
    out = jnp.concatenate([y_lru, o], axis=-1) @ w_out
    return out, k, v, h_last, new_buf


def pool_mixer(h, buf, pos0, pool_w, pool_scale):
    B, T, C = h.shape
    xp_raw = jnp.concatenate([buf, h], axis=1)
    xp = xp_raw.astype(jnp.float32)
    cs = jnp.concatenate([jnp.zeros((B, 1, C), jnp.float32), jnp.cumsum(xp, axis=1)], axis=1)
    pos = pos0 + jnp.arange(T)
    groups = []
    for g, w in enumerate(POOL_WINDOWS):
        lo, hi = g * POOL_GD, (g + 1) * POOL_GD
        s = cs[:, POOL_BUF + 1:POOL_BUF + 1 + T, lo:hi] - cs[:, POOL_BUF + 1 - w:POOL_BUF + 1 - w + T, lo:hi]
        cnt = jnp.minimum(w, pos + 1).astype(jnp.float32)[None, :, None]
        groups.append(s / cnt)
    pooled = jnp.concatenate(groups, axis=-1).astype(h.dtype) - h
    y = jnp.einsum('btgi,gij->btgj', pooled.reshape(B, T, POOL_GROUPS, POOL_GD), pool_w).reshape(B, T, C)
    return y * pool_scale, xp_raw[:, xp_raw.shape[1] - POOL_BUF:]


def swiglu(h, wg, wu, wd):
    return (jax.nn.silu(h @ wg) * (h @ wu)) @ wd


def moe(h, router_w, router_b, wg, wu, wd):
    logits = (h @ router_w + router_b).astype(jnp.float32)
    top_v, top_i = lax.top_k(logits, TOP_K)
    gates = jax.nn.softmax(top_v, axis=-1)
    w_e = jnp.sum(jax.nn.one_hot(top_i, N_EXPERTS, dtype=jnp.float32) * gates[..., None], axis=-2)
    out = jnp.zeros_like(h)
    for e in range(N_EXPERTS):
        out = out + w_e[..., e:e + 1].astype(h.dtype) * swiglu(h, wg[e], wu[e], wd[e])
    return out


def trunk(x, c, pos0, cache_k, cache_v, st_h, st_conv, st_pool, p):
    B = x.shape[0]
    cs = jax.nn.silu(c)
    new_k, new_v, new_h, new_conv, new_pool = [], [], [], [], []
    for l in range(DEPTH):
        e = l // 2
        mod = cs @ p['ada_w'][l] + p['ada_b'][l]
        sh_m, sc_m, g_m, sh_f, sc_f, g_f = jnp.split(mod[:, None, :], 6, axis=-1)
        h = rms_norm(x, p['g_mix_pre'][l]) * (1.0 + sc_m) + sh_m
        if l % 2 == 0:
            if cache_k is None:
                conv_buf = jnp.zeros((B, CONV_W - 1, LRU_WIDTH), x.dtype)
                h0 = jnp.zeros((B, LRU_WIDTH), x.dtype)
                kc = None
                vc = None
            else:
                conv_buf, h0, kc, vc = st_conv[e], st_h[e], cache_k[e], cache_v[e]
            y, k, v, h_last, cb = mixer_ab(h, conv_buf, h0, kc, vc, p['w_in'][e], p['w_out'][e],
                                           p['conv_w'][e], p['conv_b'][e], p['lru_wa'][e], p['lru_ba'][e],
                                           p['lru_wx'][e], p['lru_bx'][e], p['lru_lambda'][e])
            new_k.append(k)
            new_v.append(v)
            new_h.append(h_last)
            new_conv.append(cb)
        else:
            pbuf = jnp.zeros((B, POOL_BUF, D_MODEL), x.dtype) if st_pool is None else st_pool[e]
            y, pb = pool_mixer(h, pbuf, pos0, p['pool_w'][e], p['pool_scale'][e])
            new_pool.append(pb)
        x = x + g_m * rms_norm(y, p['g_mix_post'][l])
        h = rms_norm(x, p['g_ffn_pre'][l]) * (1.0 + sc_f) + sh_f
        if l % 2 == 0:
            y = swiglu(h, p['ffn_w_gate'][e], p['ffn_w_up'][e], p['ffn_w_down'][e])
        else:
            y = moe(h, p['router_w'][e], p['router_b'][e], p['moe_w_gate'][e], p['moe_w_up'][e], p['moe_w_down'][e])
        x = x + g_f * rms_norm(y, p['g_ffn_post'][l])
    return (x, jnp.stack(new_k), jnp.stack(new_v), jnp.stack(new_h), jnp.stack(new_conv), jnp.stack(new_pool))


def setup_inputs(seed: int = 0) -> dict:
    key = jax.random.key(seed)
    keys = list(jax.random.split(key, 40))

    def nrm(shape, scale=1.0):
        return jax.random.normal(keys.pop(), shape, jnp.float32) * scale

    def gain(shape):
        return 1.0 + nrm(shape, 0.02)

    u = jax.random.uniform(keys.pop(), (N_EVEN, LRU_WIDTH), jnp.float32, 0.9, 0.999)
    a0 = u ** (1.0 / LRU_C)
    lru_lambda = jnp.log(a0) - jnp.log1p(-a0)
    return {
        'x_prompt': nrm((BATCH, SEQ, D_MODEL)),
        'x_sample': nrm((DEC_BATCH, DEC_SEQ, D_MODEL)),
        'cache_sb_k': nrm((N_EVEN, DEC_BATCH, PAST_LEN, SB_HEADS, SB_HD)),
        'cache_sb_v': nrm((N_EVEN, DEC_BATCH, PAST_LEN, SB_HEADS, SB_HD)),
        'state_lru_h': nrm((N_EVEN, DEC_BATCH, LRU_WIDTH), 0.5),
        'state_lru_conv': nrm((N_EVEN, DEC_BATCH, CONV_W - 1, LRU_WIDTH)),
        'state_pool': nrm((N_ODD, DEC_BATCH, POOL_BUF, D_MODEL)),
        'c_prompt': nrm((BATCH, D_MODEL)),
        'c_sample': nrm((DEC_BATCH, D_MODEL)),
        'ada_w': nrm((DEPTH, D_MODEL, 6 * D_MODEL), 0.5 * D_MODEL ** -0.5),
        'ada_b': nrm((DEPTH, 6 * D_MODEL), 0.02),
        'g_mix_pre': gain((DEPTH, D_MODEL)),
        'g_mix_post': gain((DEPTH, D_MODEL)),
        'g_ffn_pre': gain((DEPTH, D_MODEL)),
        'g_ffn_post': gain((DEPTH, D_MODEL)),
        'w_in': nrm((N_EVEN, D_MODEL, IN_COLS), D_MODEL ** -0.5),
        'w_out': nrm((N_EVEN, MIX_WIDTH, D_MODEL), MIX_WIDTH ** -0.5),
        'conv_w': nrm((N_EVEN, CONV_W, LRU_WIDTH), CONV_W ** -0.5),
        'conv_b': nrm((N_EVEN, LRU_WIDTH), 0.02),
        'lru_wa': nrm((N_EVEN, LRU_HEADS, LRU_HD, LRU_HD), LRU_HD ** -0.5),
        'lru_ba': nrm((N_EVEN, LRU_WIDTH), 0.02),
        'lru_wx': nrm((N_EVEN, LRU_HEADS, LRU_HD, LRU_HD), LRU_HD ** -0.5),
        'lru_bx': nrm((N_EVEN, LRU_WIDTH), 0.02),
        'lru_lambda': lru_lambda,
        'pool_w': nrm((N_ODD, POOL_GROUPS, POOL_GD, POOL_GD), POOL_GD ** -0.5),
        'pool_scale': 1.0 + nrm((N_ODD, D_MODEL), 0.1),
        'ffn_w_gate': nrm((N_EVEN, D_MODEL, D_FF), D_MODEL ** -0.5),
        'ffn_w_up': nrm((N_EVEN, D_MODEL, D_FF), D_MODEL ** -0.5),
        'ffn_w_down': nrm((N_EVEN, D_FF, D_MODEL), D_FF ** -0.5),
        'router_w': nrm((N_ODD, D_MODEL, N_EXPERTS), D_MODEL ** -0.5),
        'router_b': nrm((N_ODD, N_EXPERTS), 0.01),
        'moe_w_gate': nrm((N_ODD, N_EXPERTS, D_MODEL, D_FF_EXPERT), D_MODEL ** -0.5),
        'moe_w_up': nrm((N_ODD, N_EXPERTS, D_MODEL, D_FF_EXPERT), D_MODEL ** -0.5),
        'moe_w_down': nrm((N_ODD, N_EXPERTS, D_FF_EXPERT, D_MODEL), D_FF_EXPERT ** -0.5),
    }


def reference(x_prompt, x_sample, cache_sb_k, cache_sb_v, state_lru_h, state_lru_conv, state_pool,
              c_prompt, c_sample, ada_w, ada_b, g_mix_pre, g_mix_post, g_ffn_pre, g_ffn_post,
              w_in, w_out, conv_w, conv_b, lru_wa, lru_ba, lru_wx, lru_bx, lru_lambda,
              pool_w, pool_scale, ffn_w_gate, ffn_w_up, ffn_w_down,
              router_w, router_b, moe_w_gate, moe_w_up, moe_w_down):
    p = dict(ada_w=ada_w, ada_b=ada_b, g_mix_pre=g_mix_pre, g_mix_post=g_mix_post,
             g_ffn_pre=g_ffn_pre, g_ffn_post=g_ffn_post, w_in=w_in, w_out=w_out,
             conv_w=conv_w, conv_b=conv_b, lru_wa=lru_wa, lru_ba=lru_ba, lru_wx=lru_wx,
             lru_bx=lru_bx, lru_lambda=lru_lambda, pool_w=pool_w, pool_scale=pool_scale,
             ffn_w_gate=ffn_w_gate, ffn_w_up=ffn_w_up, ffn_w_down=ffn_w_down,
             router_w=router_w, router_b=router_b, moe_w_gate=moe_w_gate,
             moe_w_up=moe_w_up, moe_w_down=moe_w_down)
    y_prompt, sb_k_prompt, sb_v_prompt, lru_h_prompt, lru_conv_prompt, pool_prompt = trunk(
        x_prompt, c_prompt, 0, None, None, None, None, None, p)
    y_sample, sb_k_sample, sb_v_sample, lru_h_sample, lru_conv_sample, pool_sample = trunk(
        x_sample, c_sample, cache_sb_k.shape[2], cache_sb_k, cache_sb_v, state_lru_h,
        state_lru_conv, state_pool, p)
    return (y_prompt, y_sample, sb_k_prompt, sb_v_prompt, lru_h_prompt, lru_conv_prompt, pool_prompt,
            sb_k_sample, sb_v_sample, lru_h_sample, lru_conv_sample, pool_sample)
```

```python
import functools

import jax
import jax.numpy as jnp
from jax import lax
from jax.experimental import pallas as pl
from jax.experimental.pallas import tpu as pltpu

F32 = jnp.float32
BF16 = jnp.bfloat16

EPS = 1e-6
LRU_C = 8.0
CONV_W = 4
SB_HD = 64
POOL_WINDOWS = (2, 4, 8, 16)
POOL_BUF = max(POOL_WINDOWS) - 1
TOP_K = 2
LANES = 128
FF_CHUNK = 256

VMEM_BIG = 56 * 1024 * 1024
VMEM_MID = 40 * 1024 * 1024


def _cparams(n_axes, vmem=None):
    return pltpu.CompilerParams(dimension_semantics=("arbitrary",) * n_axes, vmem_limit_bytes=vmem)


def _rms(x, g):
    return x * lax.rsqrt(jnp.mean(x * x, axis=-1, keepdims=True) + EPS) * g


def _silu(x):
    return x * jax.nn.sigmoid(x)


def _gelu_tanh(x):
    return 0.5 * x * (1.0 + jnp.tanh(0.7978845608028654 * (x + 0.044715 * (x * x * x))))


def _dot(a, b):
    return jnp.dot(a, b, preferred_element_type=F32)


def _dot_nt(a, b):
    return lax.dot_general(a, b, (((1,), (1,)), ((), ())), preferred_element_type=F32)


def _const_spec(shape, single=False):
    nd = len(shape)
    if single:
        return pl.BlockSpec(shape, lambda *_: (0,) * nd, pipeline_mode=pl.Buffered(1))
    return pl.BlockSpec(shape, lambda *_: (0,) * nd)


def _split(a):
    hi = a.astype(BF16)
    return hi, (a - hi.astype(F32)).astype(BF16)


def _act(a, hi):
    return _split(a) if hi else (a.astype(BF16), None)


def _mm(act, w, rows=slice(None), cols=slice(None)):
    ah, al = act
    wh = w[0][rows, cols]
    out = _dot(ah, wh)
    if al is not None:
        out = out + (_dot(al, wh) + _dot(ah, w[1][rows, cols]))
    return out


def _take_w(refs, hi):
    if hi:
        return (refs[0], refs[1]), refs[2:]
    return (refs[0], None), refs[1:]


def _w_args(w, hi):
    return list(w) if hi else [w[0]]


def _w_specs(w, hi, single=True):
    return [_const_spec(a.shape, single) for a in _w_args(w, hi)]


def _ada_kernel(c_ref, w_ref, b_ref, o_ref):
    ch, cl = _split(_silu(c_ref[...]))
    wh, wl = _split(w_ref[...])
    o_ref[...] = _dot(ch, wh) + (_dot(cl, wh) + _dot(ch, wl)) + b_ref[...]


def _ada_mod(c_all, ada_w, ada_b, tn=1536):
    depth, d, n6 = ada_w.shape
    r = c_all.shape[0]
    return pl.pallas_call(
        _ada_kernel,
        grid=(depth, n6 // tn),
        in_specs=[pl.BlockSpec((r, d), lambda l, j: (0, 0)),
                  pl.BlockSpec((None, d, tn), lambda l, j: (l, 0, j)),
                  pl.BlockSpec((None, 1, tn), lambda l, j: (l, 0, j))],
        out_specs=pl.BlockSpec((None, r, tn), lambda l, j: (l, 0, j)),
        out_shape=jax.ShapeDtypeStruct((depth, r, n6), F32),
        compiler_params=_cparams(2, VMEM_MID),
        name="ada_mod",
    )(c_all, ada_w, ada_b.reshape(depth, 1, n6))


def _in_proj_kernel(x_ref, mod_ref, g_ref, *refs, lw, sw, hi):
    w, outs = _take_w(refs, hi)
    nb, tt, d = x_ref.shape
    m = mod_ref[...]
    h = _rms(x_ref[...], g_ref[...]) * (1.0 + m[:, 1:2, :]) + m[:, 0:1, :]
    act = _act(h.reshape(nb * tt, d), hi)

    def proj(c0, c1):
        return _mm(act, w, cols=slice(c0, c1)).reshape(nb, tt, c1 - c0)

    outs[0][...] = proj(0, lw)
    outs[1][...] = proj(lw, 2 * lw)
    outs[2][...] = (proj(2 * lw, 2 * lw + sw) * (SB_HD ** -0.5)).astype(outs[2].dtype)
    k = proj(2 * lw + sw, 2 * lw + 2 * sw)
    v = proj(2 * lw + 2 * sw, 2 * lw + 3 * sw)
    outs[3][...] = k
    outs[4][...] = v
    if not hi:
        outs[5][...] = k.astype(BF16)
        outs[6][...] = v.astype(BF16)


def _in_proj(x, mod, g, w, nb, tt, lw, sw, hi):
    b, t, d = x.shape
    row = lambda i, j: (i, j, 0)
    spec = lambda n: pl.BlockSpec((nb, tt, n), row)
    sds = lambda n, dt: jax.ShapeDtypeStruct((b, t, n), dt)
    out_shape = [sds(lw, F32), sds(lw, F32), sds(sw, F32 if hi else BF16), sds(sw, F32), sds(sw, F32)]
    out_specs = [spec(lw), spec(lw), spec(sw), spec(sw), spec(sw)]
    if not hi:
        out_shape += [sds(sw, BF16), sds(sw, BF16)]
        out_specs += [spec(sw), spec(sw)]
    return pl.pallas_call(
        functools.partial(_in_proj_kernel, lw=lw, sw=sw, hi=hi),
        grid=(b // nb, t // tt),
        in_specs=[spec(d), pl.BlockSpec((nb, 6, d), lambda i, j: (i, 0, 0)), _const_spec((1, d))]
        + _w_specs(w, hi),
        out_specs=out_specs,
        out_shape=out_shape,
        compiler_params=_cparams(2, VMEM_MID),
        name="in_proj",
    )(x, mod, g, *_w_args(w, hi))


def _lru_kernel(xin_ref, gate_ref, cbuf_ref, h0_ref, cw_ref, cb_ref, ba_ref, bx_ref, lam_ref, *refs, hi):
    wa, refs = _take_w(refs, hi)
    wx, refs = _take_w(refs, hi)
    y_ref, hlast_ref, cout_ref, xp_s, a_s, u_s, hs_s, hcar_s = refs
    tt = xin_ref.shape[1]
    pad = 8

    @pl.when(pl.program_id(1) == 0)
    def _():
        xp_s[pad - (CONV_W - 1):pad, :] = cbuf_ref[0]
        hcar_s[...] = h0_ref[0]

    x = xin_ref[0]
    xp_s[pad:pad + tt, :] = x
    w = cw_ref[...]
    xc = cb_ref[...] + xp_s[pad - 3:pad - 3 + tt, :] * w[0:1]
    xc = xc + xp_s[pad - 2:pad - 2 + tt, :] * w[1:2]
    xc = xc + xp_s[pad - 1:pad - 1 + tt, :] * w[2:3]
    xc = xc + x * w[3:4]
    tail = xp_s[pad + tt - (CONV_W - 1):pad + tt, :]
    xp_s[pad - (CONV_W - 1):pad, :] = tail
    cout_ref[0] = tail

    act = _act(xc, hi)
    r = jax.nn.sigmoid(_mm(act, wa) + ba_ref[...])
    i = jax.nn.sigmoid(_mm(act, wx) + bx_ref[...])
    nl = -lam_ref[...]
    softplus = jnp.maximum(nl, 0.0) + jnp.log1p(jnp.exp(-jnp.abs(nl)))
    log_a = (-LRU_C) * r * softplus
    a = jnp.exp(log_a)
    a_s[...] = a
    u_s[...] = jnp.sqrt(jnp.tanh(-log_a) * (a * a + 1.0)) * (i * xc)

    def step(s, h):
        h = a_s[pl.ds(s, 1), :] * h + u_s[pl.ds(s, 1), :]
        hs_s[pl.ds(s, 1), :] = h
        return h

    h = lax.fori_loop(0, tt, step, hcar_s[...], unroll=8)
    hcar_s[...] = h
    hlast_ref[0] = h
    y_ref[0] = (hs_s[...] * _gelu_tanh(gate_ref[0])).astype(y_ref.dtype)


def _lru(lru_in, gate, cbuf, h0, conv_w, conv_b, wa, ba, wx, bx, lam, tt, hi):
    b, t, c = lru_in.shape
    row = lambda i, j: (i, j, 0)
    per_b = lambda i, j: (i, 0, 0)
    vec = lambda a: a.reshape(1, c)
    return pl.pallas_call(
        functools.partial(_lru_kernel, hi=hi),
        grid=(b, t // tt),
        in_specs=[pl.BlockSpec((1, tt, c), row), pl.BlockSpec((1, tt, c), row),
                  pl.BlockSpec((1, CONV_W - 1, c), per_b), pl.BlockSpec((1, 1, c), per_b),
                  _const_spec((CONV_W, c)), _const_spec((1, c)), _const_spec((1, c)), _const_spec((1, c)),
                  _const_spec((1, c))] + _w_specs(wa, hi, False) + _w_specs(wx, hi, False),
        out_specs=[pl.BlockSpec((1, tt, c), row), pl.BlockSpec((1, 1, c), per_b),
                   pl.BlockSpec((1, CONV_W - 1, c), per_b)],
        out_shape=[jax.ShapeDtypeStruct((b, t, c), F32 if hi else BF16), jax.ShapeDtypeStruct((b, 1, c), F32),
                   jax.ShapeDtypeStruct((b, CONV_W - 1, c), F32)],
        scratch_shapes=[pltpu.VMEM((tt + 8, c), F32), pltpu.VMEM((tt, c), F32), pltpu.VMEM((tt, c), F32),
                        pltpu.VMEM((tt, c), F32), pltpu.VMEM((1, c), F32)],
        compiler_params=_cparams(2, VMEM_MID),
        name="rg_lru",
    )(lru_in, gate, cbuf, h0.reshape(b, 1, c), conv_w, vec(conv_b), vec(ba), vec(bx), vec(lam),
      *_w_args(wa, hi), *_w_args(wx, hi))


def _sb_tile(q, k, v, r_in, u, mask, hi):
    z = _dot_nt(q[0], k[0])
    if hi:
        z = z + (_dot_nt(q[1], k[0]) + _dot_nt(q[0], k[1]))
    soft = jnp.log(1.0 + jnp.exp(-jnp.abs(z)))
    log_om = jnp.minimum(-z, 0.0) - soft
    log_beta = log_om + z
    if mask is not None:
        log_om = jnp.where(mask, log_om, 0.0)
    lh, ll = _split(log_om)
    stick = _dot(lh, u) + _dot(ll, u) + r_in
    a = jnp.exp(log_beta + stick)
    if mask is not None:
        a = jnp.where(mask, a, 0.0)
    ah, al = _act(a, hi)
    o = _dot(ah, v[0])
    if hi:
        o = o + (_dot(al, v[0]) + _dot(ah, v[1]))
    return o, r_in + jnp.sum(log_om, axis=1, keepdims=True)


def _attn_prompt_kernel(q_ref, k_ref, v_ref, u_ref, o_ref, *, tile, hi):
    qi = pl.program_id(2)
    q = q_ref[0]
    u = u_ref[...]
    lane = lax.broadcasted_iota(jnp.int32, (tile, LANES), 1)
    row = lax.broadcasted_iota(jnp.int32, (tile, tile), 0)
    col = lax.broadcasted_iota(jnp.int32, (tile, tile), 1)
    diag_mask = col < row

    def kv(j):
        start = pl.multiple_of(j * tile, tile)
        return _act(k_ref[0, pl.ds(start, tile), :], hi), _act(v_ref[0, pl.ds(start, tile), :], hi)

    accs = []
    for h in range(LANES // SB_HD):
        in_head = (lane >= h * SB_HD) & (lane < (h + 1) * SB_HD)
        qh = _act(jnp.where(in_head, q, jnp.zeros_like(q)), hi)
        kt, vt = kv(qi)
        acc, r = _sb_tile(qh, kt, vt, jnp.zeros((tile, 1), F32), u, diag_mask, hi)

        def body(jj, carry, qh=qh):
            acc, r = carry
            kt, vt = kv(qi - 1 - jj)
            o, r = _sb_tile(qh, kt, vt, r, u, None, hi)
            return acc + o, r

        acc, r = lax.fori_loop(0, qi, body, (acc, r))
        accs.append(acc)
    o_ref[0] = jnp.where(lane < SB_HD, accs[0], accs[1]).astype(o_ref.dtype)


def _upper_ones(n):
    j = lax.broadcasted_iota(jnp.int32, (n, n), 0)
    s = lax.broadcasted_iota(jnp.int32, (n, n), 1)
    return (j > s).astype(BF16)


def _attn_prompt(q, k, v, tile, hi):
    b, t, sw = q.shape
    return pl.pallas_call(
        functools.partial(_attn_prompt_kernel, tile=tile, hi=hi),
        grid=(b, sw // LANES, t // tile),
        in_specs=[pl.BlockSpec((1, tile, LANES), lambda i, hp, qi: (i, qi, hp)),
                  pl.BlockSpec((1, t, LANES), lambda i, hp, qi: (i, 0, hp)),
                  pl.BlockSpec((1, t, LANES), lambda i, hp, qi: (i, 0, hp)),
                  _const_spec((tile, tile))],
        out_specs=pl.BlockSpec((1, tile, LANES), lambda i, hp, qi: (i, qi, hp)),
        out_shape=jax.ShapeDtypeStruct((b, t, sw), F32 if hi else BF16),
        compiler_params=_cparams(3, VMEM_MID),
        name="sb_attn_prompt",
    )(q, k, v, _upper_ones(tile))


def _attn_sample_kernel(q_ref, kn_ref, vn_ref, kc_ref, vc_ref, u_ref, o_ref, *, tile, hi):
    tq, sw = q_ref.shape[1], q_ref.shape[2]
    heads = sw // SB_HD
    past = kc_ref.shape[1]
    rows = heads * tq
    q = q_ref[0]
    lane = lax.broadcasted_iota(jnp.int32, (rows, sw), 1)
    row = lax.broadcasted_iota(jnp.int32, (rows, sw), 0)
    keep = lane // SB_HD == row // tq
    qbd = _act(jnp.where(keep, jnp.concatenate([q] * heads, axis=0), jnp.zeros((), q.dtype)), hi)
    u = u_ref[...]

    zpad = jnp.zeros((LANES - tq, sw), kn_ref.dtype)
    kn = _act(jnp.concatenate([kn_ref[0], zpad], axis=0), hi)
    vn = _act(jnp.concatenate([vn_ref[0], zpad], axis=0), hi)
    r_new = lax.broadcasted_iota(jnp.int32, (rows, LANES), 0)
    c_new = lax.broadcasted_iota(jnp.int32, (rows, LANES), 1)
    new_mask = c_new < (r_new % tq)
    acc, r = _sb_tile(qbd, kn, vn, jnp.zeros((rows, 1), F32), u[:LANES, :LANES], new_mask, hi)
    for j in range(past // tile - 1, -1, -1):
        kt = _act(kc_ref[0, j * tile:(j + 1) * tile, :], hi)
        vt = _act(vc_ref[0, j * tile:(j + 1) * tile, :], hi)
        o, r = _sb_tile(qbd, kt, vt, r, u, None, hi)
        acc = acc + o
    acc = jnp.where(keep, acc, 0.0)
    out = acc[0:tq]
    for h in range(1, heads):
        out = out + acc[h * tq:(h + 1) * tq]
    o_ref[0] = out.astype(o_ref.dtype)


def _attn_sample(qb, kb, vb, cache_k, cache_v, tile, hi):
    b, tq, sw = qb.shape
    past = cache_k.shape[1]
    per_b = lambda i: (i, 0, 0)
    return pl.pallas_call(
        functools.partial(_attn_sample_kernel, tile=tile, hi=hi),
        grid=(b,),
        in_specs=[pl.BlockSpec((1, tq, sw), per_b), pl.BlockSpec((1, tq, sw), per_b),
                  pl.BlockSpec((1, tq, sw), per_b),
                  pl.BlockSpec((1, past, sw), per_b), pl.BlockSpec((1, past, sw), per_b),
                  _const_spec((tile, tile))],
        out_specs=pl.BlockSpec((1, tq, sw), per_b),
        out_shape=jax.ShapeDtypeStruct((b, tq, sw), F32 if hi else BF16),
        compiler_params=_cparams(1, VMEM_MID),
        name="sb_attn_sample",
    )(qb, kb, vb, cache_k, cache_v, _upper_ones(tile))


def _swiglu_rows(act, wg, wu, wd, hi):
    ff = wg[0].shape[-1]
    acc = None
    for c0 in range(0, ff, FF_CHUNK):
        cs = slice(c0, c0 + FF_CHUNK)
        g = _mm(act, wg, cols=cs)
        up = _mm(act, wu, cols=cs)
        part = _mm(_act(_silu(g) * up, hi), wd, rows=cs)
        acc = part if acc is None else acc + part
    return acc


def _out_ffn_kernel(x_ref, yl_ref, oa_ref, mod_ref, gmp_ref, gfp_ref, gfo_ref, *refs, hi):
    wo, refs = _take_w(refs, hi)
    wg, refs = _take_w(refs, hi)
    wu, refs = _take_w(refs, hi)
    wd, refs = _take_w(refs, hi)
    out_ref, = refs
    nb, tt, d = x_ref.shape
    lw = yl_ref.shape[2]
    m = mod_ref[...]
    yl = _act(yl_ref[...].reshape(nb * tt, lw), hi)
    oa = _act(oa_ref[...].reshape(nb * tt, oa_ref.shape[2]), hi)
    y = _mm(yl, wo, rows=slice(0, lw)) + _mm(oa, wo, rows=slice(lw, None))
    x1 = x_ref[...] + m[:, 2:3, :] * _rms(y.reshape(nb, tt, d), gmp_ref[...])
    h = _rms(x1, gfp_ref[...]) * (1.0 + m[:, 4:5, :]) + m[:, 3:4, :]
    f = _swiglu_rows(_act(h.reshape(nb * tt, d), hi), wg, wu, wd, hi)
    out_ref[...] = x1 + m[:, 5:6, :] * _rms(f.reshape(nb, tt, d), gfo_ref[...])


def _out_ffn(x, yl, oa, mod, g_mix_post, g_ffn_pre, g_ffn_post, wo, wg, wu, wd, nb, tt, hi):
    b, t, d = x.shape
    row = lambda i, j: (i, j, 0)
    weights = [wo, wg, wu, wd]
    return pl.pallas_call(
        functools.partial(_out_ffn_kernel, hi=hi),
        grid=(b // nb, t // tt),
        in_specs=[pl.BlockSpec((nb, tt, d), row), pl.BlockSpec((nb, tt, yl.shape[2]), row),
                  pl.BlockSpec((nb, tt, oa.shape[2]), row),
                  pl.BlockSpec((nb, 6, d), lambda i, j: (i, 0, 0)),
                  _const_spec((1, d)), _const_spec((1, d)), _const_spec((1, d))]
        + [s for w in weights for s in _w_specs(w, hi)],
        out_specs=pl.BlockSpec((nb, tt, d), row),
        out_shape=jax.ShapeDtypeStruct((b, t, d), F32),
        compiler_params=_cparams(2, VMEM_BIG),
        name="out_proj_ffn",
    )(x, yl, oa, mod, g_mix_post, g_ffn_pre, g_ffn_post, *[a for w in weights for a in _w_args(w, hi)])


def _pool_router_kernel(x_ref, mod_ref, gpre_ref, pbuf_ref, ps_ref, gmp_ref, gfp_ref,
                        rwh_ref, rwl_ref, rb_ref, ltri_ref, *refs, pos0, n_experts, hi):
    pw, refs = _take_w(refs, hi)
    x1_ref, h2_ref, meta_ref, pout_ref, cnt_ref, hp_s, cnt_s = refs
    tt, d = x_ref.shape[1], x_ref.shape[2]
    hal = POOL_BUF + 1
    bi, ti = pl.program_id(0), pl.program_id(1)

    @pl.when((bi == 0) & (ti == 0))
    def _():
        cnt_s[...] = jnp.zeros_like(cnt_s)

    @pl.when(ti == 0)
    def _():
        hp_s[0:1, :] = jnp.zeros((1, d), F32)
        hp_s[1:hal, :] = pbuf_ref[0]

    m = mod_ref[0]
    x = x_ref[0]
    h = _rms(x, gpre_ref[...]) * (1.0 + m[1:2, :]) + m[0:1, :]
    hp_s[hal:hal + tt, :] = h

    pos = pos0 + ti * tt + lax.broadcasted_iota(jnp.int32, (tt, 1), 0)
    gd = d // len(POOL_WINDOWS)
    ys = []
    for g, w in enumerate(POOL_WINDOWS):
        lo = g * gd
        s = hp_s[hal:hal + tt, lo:lo + gd]
        for back in range(1, w):
            s = s + hp_s[hal - back:hal - back + tt, lo:lo + gd]
        cnt = jnp.minimum(w, pos + 1).astype(F32)
        pooled = s / cnt - h[:, lo:lo + gd]
        ys.append(_mm(_act(pooled, hi), (pw[0].at[g], None if pw[1] is None else pw[1].at[g])))
    y = jnp.concatenate(ys, axis=1) * ps_ref[...]
    tail = hp_s[hal + tt - POOL_BUF:hal + tt, :]
    hp_s[1:hal, :] = tail
    pout_ref[0] = tail

    x1 = x + m[2:3, :] * _rms(y, gmp_ref[...])
    x1_ref[0] = x1
    h2 = _rms(x1, gfp_ref[...]) * (1.0 + m[4:5, :]) + m[3:4, :]
    h2_ref[0] = h2

    logits = _mm(_split(h2), (rwh_ref, rwl_ref)) + rb_ref[...]
    lane = lax.broadcasted_iota(jnp.int32, (tt, LANES), 1)
    neg = jnp.float32(-jnp.inf)
    l1 = jnp.where(lane < n_experts, logits, neg)
    m1 = jnp.max(l1, axis=1, keepdims=True)
    i1 = jnp.min(jnp.where(l1 == m1, lane, LANES), axis=1, keepdims=True)
    l2 = jnp.where(lane == i1, neg, l1)
    m2 = jnp.max(l2, axis=1, keepdims=True)
    i2 = jnp.min(jnp.where(l2 == m2, lane, LANES), axis=1, keepdims=True)
    e21 = jnp.exp(m2 - m1)
    g1 = 1.0 / (1.0 + e21)
    g2 = e21 * g1
    sel = ((lane == i1) | (lane == i2)).astype(F32)
    rank = _dot(ltri_ref[...], sel.astype(BF16)) + cnt_s[...]
    r1 = jnp.sum(jnp.where(lane == i1, rank, 0.0), axis=1, keepdims=True)
    r2 = jnp.sum(jnp.where(lane == i2, rank, 0.0), axis=1, keepdims=True)
    cnt_new = cnt_s[...] + jnp.sum(sel, axis=0, keepdims=True)
    cnt_s[...] = cnt_new
    cnt_ref[...] = jnp.broadcast_to(cnt_new, cnt_ref.shape)
    meta = jnp.where(lane == 0, i1.astype(F32), 0.0)
    meta = jnp.where(lane == 1, i2.astype(F32), meta)
    meta = jnp.where(lane == 2, g1, meta)
    meta = jnp.where(lane == 3, g2, meta)
    meta = jnp.where(lane == 4, r1, meta)
    meta = jnp.where(lane == 5, r2, meta)
    meta_ref[0] = meta


def _lower_ones(n):
    i = lax.broadcasted_iota(jnp.int32, (n, n), 0)
    j = lax.broadcasted_iota(jnp.int32, (n, n), 1)
    return (j < i).astype(BF16)


def _pool_router(x, mod, g_pre, pbuf, pool_w, pool_scale, g_mix_post, g_ffn_pre, rw_hi, rw_lo, rb, tt, pos0,
                 n_experts, hi):
    b, t, d = x.shape
    row = lambda i, j: (i, j, 0)
    per_b = lambda i, j: (i, 0, 0)
    return pl.pallas_call(
        functools.partial(_pool_router_kernel, pos0=pos0, n_experts=n_experts, hi=hi),
        grid=(b, t // tt),
        in_specs=[pl.BlockSpec((1, tt, d), row), pl.BlockSpec((1, 6, d), per_b), _const_spec((1, d)),
                  pl.BlockSpec((1, POOL_BUF, d), per_b), _const_spec((1, d)),
                  _const_spec((1, d)), _const_spec((1, d)),
                  _const_spec(rw_hi.shape), _const_spec(rw_lo.shape), _const_spec((1, LANES)),
                  _const_spec((tt, tt))] + _w_specs(pool_w, hi, False),
        out_specs=[pl.BlockSpec((1, tt, d), row), pl.BlockSpec((1, tt, d), row),
                   pl.BlockSpec((1, tt, LANES), row), pl.BlockSpec((1, POOL_BUF, d), per_b),
                   _const_spec((8, LANES))],
        out_shape=[jax.ShapeDtypeStruct((b, t, d), F32), jax.ShapeDtypeStruct((b, t, d), F32),
                   jax.ShapeDtypeStruct((b, t, LANES), F32), jax.ShapeDtypeStruct((b, POOL_BUF, d), F32),
                   jax.ShapeDtypeStruct((8, LANES), F32)],
        scratch_shapes=[pltpu.VMEM((tt + POOL_BUF + 1, d), F32), pltpu.VMEM((1, LANES), F32)],
        compiler_params=_cparams(2, VMEM_MID),
        name="pool_router",
    )(x, mod, g_pre, pbuf, pool_scale, g_mix_post, g_ffn_pre, rw_hi, rw_lo, rb, _lower_ones(tt),
      *_w_args(pool_w, hi))


def _gather_kernel(idx_ref, src_ref, o_ref, sem, *, tg):
    base = pl.program_id(0) * tg

    def row_copy(r, t):
        return pltpu.make_async_copy(src_ref.at[pl.ds(t, 1), :], o_ref.at[pl.ds(r, 1), :], sem)

    def issue(r, c):
        row_copy(r, idx_ref[base + r]).start()
        return c

    def drain(r, c):
        row_copy(r, 0).wait()
        return c

    lax.fori_loop(0, tg, issue, 0)
    lax.fori_loop(0, tg, drain, 0)


def _gather_rows(src, idx, tg):
    n = idx.shape[0]
    d = src.shape[1]
    return pl.pallas_call(
        functools.partial(_gather_kernel, tg=tg),
        grid_spec=pltpu.PrefetchScalarGridSpec(
            num_scalar_prefetch=1,
            grid=(n // tg,),
            in_specs=[pl.BlockSpec(memory_space=pl.ANY)],
            out_specs=pl.BlockSpec((tg, d), lambda i, idx_ref: (i, 0)),
            scratch_shapes=[pltpu.SemaphoreType.DMA(())]),
        out_shape=jax.ShapeDtypeStruct((n, d), src.dtype),
        compiler_params=_cparams(1),
        name="gather_rows",
    )(idx, src)


def _moe_kernel(te_ref, na_ref, xs_ref, wg_ref, wu_ref, wd_ref, ys_ref):
    active = pl.program_id(0) < na_ref[0]

    @pl.when(active)
    def _():
        ys_ref[...] = _swiglu_rows(_act(xs_ref[...], False), (wg_ref, None), (wu_ref, None), (wd_ref, None), False)

    @pl.when(jnp.logical_not(active))
    def _():
        ys_ref[...] = jnp.zeros_like(ys_ref)


def _moe_grouped(xs, tile_expert, n_active, wg_b, wu_b, wd_b, tm):
    p, d = xs.shape
    ff = wg_b.shape[2]
    rows = lambda i, te, na: (jnp.minimum(i, na[0] - 1), 0)
    expert = lambda i, te, na: (te[i], 0, 0)
    return pl.pallas_call(
        _moe_kernel,
        grid_spec=pltpu.PrefetchScalarGridSpec(
            num_scalar_prefetch=2,
            grid=(p // tm,),
            in_specs=[pl.BlockSpec((tm, d), rows),
                      pl.BlockSpec((None, d, ff), expert), pl.BlockSpec((None, d, ff), expert),
                      pl.BlockSpec((None, ff, d), expert)],
            out_specs=pl.BlockSpec((tm, d), lambda i, te, na: (i, 0))),
        out_shape=jax.ShapeDtypeStruct((p, d), F32),
        compiler_params=_cparams(1, VMEM_BIG),
        name="moe_grouped_swiglu",
    )(tile_expert, n_active, xs, wg_b, wu_b, wd_b)


def _combine_kernel(x1_ref, ya_ref, yb_ref, meta_ref, mod_ref, gfo_ref, out_ref):
    m = mod_ref[...]
    meta = meta_ref[...]
    y = meta[:, :, 2:3] * ya_ref[...] + meta[:, :, 3:4] * yb_ref[...]
    out_ref[...] = x1_ref[...] + m[:, 5:6, :] * _rms(y, gfo_ref[...])


def _combine(x1, ya, yb, meta, mod, g_ffn_post, nb, tt):
    b, t, d = x1.shape
    row = lambda i, j: (i, j, 0)
    return pl.pallas_call(
        _combine_kernel,
        grid=(b // nb, t // tt),
        in_specs=[pl.BlockSpec((nb, tt, d), row), pl.BlockSpec((nb, tt, d), row), pl.BlockSpec((nb, tt, d), row),
                  pl.BlockSpec((nb, tt, LANES), row), pl.BlockSpec((nb, 6, d), lambda i, j: (i, 0, 0)),
                  _const_spec((1, d))],
        out_specs=pl.BlockSpec((nb, tt, d), row),
        out_shape=jax.ShapeDtypeStruct((b, t, d), F32),
        compiler_params=_cparams(2, VMEM_MID),
        name="moe_combine",
    )(x1, ya, yb, meta, mod, g_ffn_post)


def _block_diag(w):
    heads, hd, _ = w.shape
    eye = jnp.eye(heads, dtype=w.dtype)
    return jnp.einsum("hij,hg->higj", w, eye).reshape(heads * hd, heads * hd)


def _hi_lo(w):
    hi = w.astype(BF16)
    return hi, (w - hi.astype(F32)).astype(BF16)


def _moe_layer(x1, h2, meta, counts, mod, g_ffn_post, wg_b, wu_b, wd_b, tm, nb, tt):
    b, t, d = x1.shape
    n = b * t
    n_experts = wg_b.shape[0]
    n_tiles = (TOP_K * n) // tm + n_experts
    meta2 = meta.reshape(n, LANES)
    e_idx = meta2[:, 0:2].astype(jnp.int32)
    rank = meta2[:, 4:6].astype(jnp.int32)
    cnt = counts[0, :n_experts].astype(jnp.int32)
    tiles_e = (cnt + tm - 1) // tm
    tiles_end = jnp.cumsum(tiles_e)
    group_off = (tiles_end - tiles_e) * tm
    n_active = tiles_end[-1:]
    tile_id = jnp.minimum(jnp.arange(n_tiles, dtype=jnp.int32), n_active[0] - 1)
    tile_expert = jnp.minimum(jnp.searchsorted(tiles_end, tile_id, side="right"), n_experts - 1).astype(jnp.int32)
    pos = group_off[e_idx] + rank
    tok = jnp.broadcast_to(jnp.arange(n, dtype=jnp.int32)[:, None], (n, TOP_K))
    slot_tok = jnp.zeros((n_tiles * tm,), jnp.int32).at[pos.reshape(-1)].set(tok.reshape(-1))

    tg = min(512, n_tiles * tm)
    xs = _gather_rows(h2.reshape(n, d), slot_tok, tg)
    ys = _moe_grouped(xs, tile_expert, n_active.astype(jnp.int32), wg_b, wu_b, wd_b, tm)
    tg2 = min(512, n)
    ya = _gather_rows(ys, pos[:, 0], tg2).reshape(b, t, d)
    yb = _gather_rows(ys, pos[:, 1], tg2).reshape(b, t, d)
    return _combine(x1, ya, yb, meta, mod, g_ffn_post, nb, tt)


def _trunk(x, mod_all, pos0, cache_k, cache_v, st_h, st_conv, st_pool, p, cfg):
    b, t, d = x.shape
    depth = mod_all.shape[0]
    nb, tt = cfg["nb"], cfg["tt"]
    lw = p["conv_w"].shape[2]
    sw = (p["w_in"][0][0].shape[1] - 2 * lw) // 3
    heads = sw // SB_HD
    n_experts = p["moe_wg_b"].shape[1]
    row1 = lambda a: a.reshape(1, -1)
    new_k, new_v, new_h, new_conv, new_pool = [], [], [], [], []
    for l in range(depth):
        e = l // 2
        hi = l in cfg["hi_layers"]
        mod = mod_all[l].reshape(b, 6, d)
        if l % 2 == 0:
            proj = _in_proj(x, mod, row1(p["g_mix_pre"][l]), p["w_in"][e], nb, tt, lw, sw, hi)
            gate, lru_in, q, k32, v32 = proj[:5]
            if cache_k is None:
                cbuf = jnp.zeros((b, CONV_W - 1, lw), F32)
                h0 = jnp.zeros((b, lw), F32)
            else:
                cbuf, h0 = st_conv[e], st_h[e]
            yl, h_last, cb = _lru(lru_in, gate, cbuf, h0, p["conv_w"][e], p["conv_b"][e], p["wa"][e],
                                  p["lru_ba"][e], p["wx"][e], p["lru_bx"][e], p["lru_lambda"][e], cfg["lru_tt"], hi)
            kk, vv = (k32, v32) if hi else proj[5:7]
            if cache_k is None:
                oa = _attn_prompt(q, kk, vv, cfg["attn_tile"], hi)
            else:
                past = cache_k.shape[2]
                oa = _attn_sample(q, kk, vv, cache_k[e].reshape(b, past, sw),
                                  cache_v[e].reshape(b, past, sw), cfg["attn_tile"], hi)
            x = _out_ffn(x, yl, oa, mod, row1(p["g_mix_post"][l]), row1(p["g_ffn_pre"][l]),
                         row1(p["g_ffn_post"][l]), p["w_out"][e], p["ffn_wg"][e], p["ffn_wu"][e],
                         p["ffn_wd"][e], nb, cfg["hi_tt"] if hi else tt, hi)
            new_k.append(k32.reshape(b, t, heads, SB_HD))
            new_v.append(v32.reshape(b, t, heads, SB_HD))
            new_h.append(h_last.reshape(b, lw))
            new_conv.append(cb)
        else:
            pbuf = jnp.zeros((b, POOL_BUF, d), F32) if st_pool is None else st_pool[e]
            x1, h2, meta, pb, counts = _pool_router(
                x, mod, row1(p["g_mix_pre"][l]), pbuf, p["pool_w"][e], row1(p["pool_scale"][e]),
                row1(p["g_mix_post"][l]), row1(p["g_ffn_pre"][l]), p["rw_hi"][e], p["rw_lo"][e], p["rb"][e],
                cfg["pool_tt"], pos0, n_experts, hi)
            x = _moe_layer(x1, h2, meta, counts, mod, row1(p["g_ffn_post"][l]), p["moe_wg_b"][e], p["moe_wu_b"][e],
                           p["moe_wd_b"][e], cfg["moe_tm"], nb, tt)
            new_pool.append(pb)
    return (x, jnp.stack(new_k), jnp.stack(new_v), jnp.stack(new_h), jnp.stack(new_conv), jnp.stack(new_pool))


def kernel(x_prompt, x_sample, cache_sb_k, cache_sb_v, state_lru_h, state_lru_conv, state_pool, c_prompt, c_sample, ada_w, ada_b, g_mix_pre, g_mix_post, g_ffn_pre, g_ffn_post, w_in, w_out, conv_w, conv_b, lru_wa, lru_ba, lru_wx, lru_bx, lru_lambda, pool_w, pool_scale, ffn_w_gate, ffn_w_up, ffn_w_down, router_w, router_b, moe_w_gate, moe_w_up, moe_w_down):
    bp = x_prompt.shape[0]
    n_experts = router_w.shape[2]
    rw_hi, rw_lo = _hi_lo(jnp.pad(router_w, ((0, 0), (0, 0), (0, LANES - n_experts))))

    def per_layer(w):
        return [_hi_lo(w[e]) if e == 0 else (w[e].astype(BF16), None) for e in range(w.shape[0])]

    p = dict(
        g_mix_pre=g_mix_pre, g_mix_post=g_mix_post, g_ffn_pre=g_ffn_pre, g_ffn_post=g_ffn_post,
        w_in=per_layer(w_in), w_out=per_layer(w_out),
        conv_w=conv_w, conv_b=conv_b, lru_ba=lru_ba, lru_bx=lru_bx, lru_lambda=lru_lambda,
        wa=per_layer(jax.vmap(_block_diag)(lru_wa)), wx=per_layer(jax.vmap(_block_diag)(lru_wx)),
        pool_w=per_layer(pool_w), pool_scale=pool_scale,
        ffn_wg=per_layer(ffn_w_gate), ffn_wu=per_layer(ffn_w_up), ffn_wd=per_layer(ffn_w_down),
        rw_hi=rw_hi, rw_lo=rw_lo,
        rb=jnp.pad(router_b, ((0, 0), (0, LANES - n_experts))).reshape(-1, 1, LANES),
        moe_wg_b=moe_w_gate.astype(BF16), moe_wu_b=moe_w_up.astype(BF16), moe_wd_b=moe_w_down.astype(BF16),
    )
    mod_all = _ada_mod(jnp.concatenate([c_prompt, c_sample], axis=0), ada_w, ada_b)
    cfg_prompt = dict(nb=1, tt=512, hi_tt=256, lru_tt=512, pool_tt=512, attn_tile=256, moe_tm=512, hi_layers=(0, 1))
    ts = x_sample.shape[1]
    cfg_sample = dict(nb=x_sample.shape[0] // 2, tt=ts, hi_tt=ts, lru_tt=ts, pool_tt=ts, attn_tile=256, moe_tm=128,
                      hi_layers=(0, 1))
    yp, kp, vp, hp, cp, pp = _trunk(x_prompt, mod_all[:, :bp], 0, None, None, None, None, None, p, cfg_prompt)
    ys, ks, vs, hs, cs, ps = _trunk(x_sample, mod_all[:, bp:], cache_sb_k.shape[2], cache_sb_k, cache_sb_v,
                                    state_lru_h, state_lru_conv, state_pool, p, cfg_sample)
    return (yp, ys, kp, vp, hp, cp, pp, ks, vs, hs, cs, ps)
```

```python
import functools

import jax
import jax.numpy as jnp
from jax import lax
from jax.experimental import pallas as pl
from jax.experimental.pallas import tpu as pltpu

F32 = jnp.float32
BF16 = jnp.bfloat16

EPS = 1e-6
LRU_C = 8.0
CONV_W = 4
SB_HD = 64
POOL_WINDOWS = (2, 4, 8, 16)
POOL_BUF = max(POOL_WINDOWS) - 1
TOP_K = 2
LANES = 128
SUBLANES = 8
FF_CHUNK = 256
EXP_UNDERFLOW = -104.0

VMEM_BIG = 56 * 1024 * 1024
VMEM_MID = 40 * 1024 * 1024


def _cparams(n_axes, vmem=None):
    return pltpu.CompilerParams(dimension_semantics=("arbitrary",) * n_axes, vmem_limit_bytes=vmem)


def _rms(x, g):
    return x * lax.rsqrt(jnp.mean(x * x, axis=-1, keepdims=True) + EPS) * g


def _silu(x):
    return x * jax.nn.sigmoid(x)


def _gelu_tanh(x):
    return 0.5 * x * (1.0 + jnp.tanh(0.7978845608028654 * (x + 0.044715 * (x * x * x))))


def _dot(a, b):
    return jnp.dot(a, b, preferred_element_type=F32)


def _dot_nt(a, b):
    return lax.dot_general(a, b, (((1,), (1,)), ((), ())), preferred_element_type=F32)


def _const_spec(shape, single=False):
    nd = len(shape)
    if single:
        return pl.BlockSpec(shape, lambda *_: (0,) * nd, pipeline_mode=pl.Buffered(1))
    return pl.BlockSpec(shape, lambda *_: (0,) * nd)


def _split(a):
    hi = a.astype(BF16)
    return hi, (a - hi.astype(F32)).astype(BF16)


def _act(a, hi):
    return _split(a) if hi else (a.astype(BF16), None)


def _mm(act, w, rows=slice(None), cols=slice(None)):
    ah, al = act
    wh = w[0][rows, cols]
    out = _dot(ah, wh)
    if al is not None:
        out = out + (_dot(al, wh) + _dot(ah, w[1][rows, cols]))
    return out


def _take_w(refs, hi):
    if hi:
        return (refs[0], refs[1]), refs[2:]
    return (refs[0], None), refs[1:]


def _w_args(w, hi):
    return list(w) if hi else [w[0]]


def _w_specs(w, hi, single=True):
    return [_const_spec(a.shape, single) for a in _w_args(w, hi)]


def _rows_from_tiles(ref):
    return jnp.concatenate([ref[..., s, :] for s in range(SUBLANES)], axis=-1)


def _rows_to_tiles(ref, val):
    for s in range(SUBLANES):
        ref[..., s, :] = val[..., s * LANES:(s + 1) * LANES]


def _ada_kernel(c_ref, w_ref, b_ref, o_ref):
    ch, cl = _split(_silu(c_ref[...]))
    wh, wl = _split(w_ref[...])
    o_ref[...] = _dot(ch, wh) + (_dot(cl, wh) + _dot(ch, wl)) + b_ref[...]


def _ada_mod(c_all, ada_w, ada_b, tn=1536):
    depth, d, n6 = ada_w.shape
    r = c_all.shape[0]
    return pl.pallas_call(
        _ada_kernel,
        grid=(depth, n6 // tn),
        in_specs=[pl.BlockSpec((r, d), lambda l, j: (0, 0)),
                  pl.BlockSpec((None, d, tn), lambda l, j: (l, 0, j)),
                  pl.BlockSpec((None, 1, tn), lambda l, j: (l, 0, j))],
        out_specs=pl.BlockSpec((None, r, tn), lambda l, j: (l, 0, j)),
        out_shape=jax.ShapeDtypeStruct((depth, r, n6), F32),
        compiler_params=_cparams(2, VMEM_MID),
        name="ada_mod",
    )(c_all, ada_w, ada_b.reshape(depth, 1, n6))


def _in_proj_kernel(x_ref, mod_ref, g_ref, *refs, lw, sw, hi):
    w, outs = _take_w(refs, hi)
    nb, tt, d = x_ref.shape
    m = mod_ref[...]
    h = _rms(x_ref[...], g_ref[...]) * (1.0 + m[:, 1:2, :]) + m[:, 0:1, :]
    act = _act(h.reshape(nb * tt, d), hi)

    def proj(c0, c1):
        return _mm(act, w, cols=slice(c0, c1)).reshape(nb, tt, c1 - c0)

    outs[0][...] = proj(0, lw)
    outs[1][...] = proj(lw, 2 * lw)
    outs[2][...] = (proj(2 * lw, 2 * lw + sw) * (SB_HD ** -0.5)).astype(outs[2].dtype)
    k = proj(2 * lw + sw, 2 * lw + 2 * sw)
    v = proj(2 * lw + 2 * sw, 2 * lw + 3 * sw)
    outs[3][...] = k
    outs[4][...] = v
    if not hi:
        outs[5][...] = k.astype(BF16)
        outs[6][...] = v.astype(BF16)


def _in_proj(x, mod, g, w, nb, tt, lw, sw, hi):
    b, t, d = x.shape
    row = lambda i, j: (i, j, 0)
    spec = lambda n: pl.BlockSpec((nb, tt, n), row)
    sds = lambda n, dt: jax.ShapeDtypeStruct((b, t, n), dt)
    out_shape = [sds(lw, F32), sds(lw, F32), sds(sw, F32 if hi else BF16), sds(sw, F32), sds(sw, F32)]
    out_specs = [spec(lw), spec(lw), spec(sw), spec(sw), spec(sw)]
    if not hi:
        out_shape += [sds(sw, BF16), sds(sw, BF16)]
        out_specs += [spec(sw), spec(sw)]
    return pl.pallas_call(
        functools.partial(_in_proj_kernel, lw=lw, sw=sw, hi=hi),
        grid=(b // nb, t // tt),
        in_specs=[spec(d), pl.BlockSpec((nb, 6, d), lambda i, j: (i, 0, 0)), _const_spec((1, d))]
        + _w_specs(w, hi),
        out_specs=out_specs,
        out_shape=out_shape,
        compiler_params=_cparams(2, VMEM_MID),
        name="in_proj",
    )(x, mod, g, *_w_args(w, hi))


def _lru_kernel(xin_ref, gate_ref, cbuf_ref, h0_ref, cw_ref, cb_ref, ba_ref, bx_ref, lam_ref, *refs, hi):
    wa, refs = _take_w(refs, hi)
    wx, refs = _take_w(refs, hi)
    y_ref, hlast_ref, cout_ref, xp_s, a_s, u_s, hs_s, hcar_s = refs
    tt = xin_ref.shape[1]
    pad = 8

    @pl.when(pl.program_id(1) == 0)
    def _():
        xp_s[pad - (CONV_W - 1):pad, :] = cbuf_ref[0]
        hcar_s[...] = h0_ref[0]

    x = xin_ref[0]
    xp_s[pad:pad + tt, :] = x
    w = cw_ref[...]
    xc = cb_ref[...] + xp_s[pad - 3:pad - 3 + tt, :] * w[0:1]
    xc = xc + xp_s[pad - 2:pad - 2 + tt, :] * w[1:2]
    xc = xc + xp_s[pad - 1:pad - 1 + tt, :] * w[2:3]
    xc = xc + x * w[3:4]
    tail = xp_s[pad + tt - (CONV_W - 1):pad + tt, :]
    xp_s[pad - (CONV_W - 1):pad, :] = tail
    cout_ref[0] = tail

    act = _act(xc, hi)
    r = jax.nn.sigmoid(_mm(act, wa) + ba_ref[...])
    i = jax.nn.sigmoid(_mm(act, wx) + bx_ref[...])
    nl = -lam_ref[...]
    softplus = jnp.maximum(nl, 0.0) + jnp.log1p(jnp.exp(-jnp.abs(nl)))
    log_a = (-LRU_C) * r * softplus
    a = jnp.exp(log_a)
    a_s[...] = a
    u_s[...] = jnp.sqrt(jnp.tanh(-log_a) * (a * a + 1.0)) * (i * xc)

    def step(s, h):
        h = a_s[pl.ds(s, 1), :] * h + u_s[pl.ds(s, 1), :]
        hs_s[pl.ds(s, 1), :] = h
        return h

    h = lax.fori_loop(0, tt, step, hcar_s[...], unroll=8)
    hcar_s[...] = h
    hlast_ref[0] = h
    y_ref[0] = (hs_s[...] * _gelu_tanh(gate_ref[0])).astype(y_ref.dtype)


def _lru(lru_in, gate, cbuf, h0, conv_w, conv_b, wa, ba, wx, bx, lam, tt, hi):
    b, t, c = lru_in.shape
    row = lambda i, j: (i, j, 0)
    per_b = lambda i, j: (i, 0, 0)
    vec = lambda a: a.reshape(1, c)
    return pl.pallas_call(
        functools.partial(_lru_kernel, hi=hi),
        grid=(b, t // tt),
        in_specs=[pl.BlockSpec((1, tt, c), row), pl.BlockSpec((1, tt, c), row),
                  pl.BlockSpec((1, CONV_W - 1, c), per_b), pl.BlockSpec((1, 1, c), per_b),
                  _const_spec((CONV_W, c)), _const_spec((1, c)), _const_spec((1, c)), _const_spec((1, c)),
                  _const_spec((1, c))] + _w_specs(wa, hi, False) + _w_specs(wx, hi, False),
        out_specs=[pl.BlockSpec((1, tt, c), row), pl.BlockSpec((1, 1, c), per_b),
                   pl.BlockSpec((1, CONV_W - 1, c), per_b)],
        out_shape=[jax.ShapeDtypeStruct((b, t, c), F32 if hi else BF16), jax.ShapeDtypeStruct((b, 1, c), F32),
                   jax.ShapeDtypeStruct((b, CONV_W - 1, c), F32)],
        scratch_shapes=[pltpu.VMEM((tt + 8, c), F32), pltpu.VMEM((tt, c), F32), pltpu.VMEM((tt, c), F32),
                        pltpu.VMEM((tt, c), F32), pltpu.VMEM((1, c), F32)],
        compiler_params=_cparams(2, VMEM_MID),
        name="rg_lru",
    )(lru_in, gate, cbuf, h0.reshape(b, 1, c), conv_w, vec(conv_b), vec(ba), vec(bx), vec(lam),
      *_w_args(wa, hi), *_w_args(wx, hi))


def _sb_tile(q, k, v, r_in, u, mask, hi):
    z = _dot_nt(q[0], k[0])
    if hi:
        z = z + (_dot_nt(q[1], k[0]) + _dot_nt(q[0], k[1]))
    soft = jnp.log(1.0 + jnp.exp(-jnp.abs(z)))
    log_om = jnp.minimum(-z, 0.0) - soft
    log_beta = log_om + z
    if mask is not None:
        log_om = jnp.where(mask, log_om, 0.0)
    lh, ll = _split(log_om)
    stick = _dot(lh, u) + _dot(ll, u) + r_in
    a = jnp.exp(log_beta + stick)
    if mask is not None:
        a = jnp.where(mask, a, 0.0)
    ah, al = _act(a, hi)
    o = _dot(ah, v[0])
    if hi:
        o = o + (_dot(al, v[0]) + _dot(ah, v[1]))
    return o, r_in + jnp.sum(log_om, axis=1, keepdims=True)


def _attn_prompt_kernel(q_ref, k_ref, v_ref, u_ref, o_ref, *, tile, hi):
    qi = pl.program_id(2)
    n_slab = q_ref.shape[2] // LANES
    per_slab = LANES // SB_HD
    u = u_ref[...]
    lane = lax.broadcasted_iota(jnp.int32, (tile, LANES), 1)
    row = lax.broadcasted_iota(jnp.int32, (tile, tile), 0)
    col = lax.broadcasted_iota(jnp.int32, (tile, tile), 1)
    diag_mask = col < row

    qs = []
    for s in range(n_slab):
        q = q_ref[0, :, s * LANES:(s + 1) * LANES]
        for h in range(per_slab):
            in_head = (lane >= h * SB_HD) & (lane < (h + 1) * SB_HD)
            qs.append(_act(jnp.where(in_head, q, jnp.zeros_like(q)), hi))

    def tiles(j, rs, mask):
        start = pl.multiple_of(j * tile, tile)
        outs = []
        for s in range(n_slab):
            kt = _act(k_ref[0, pl.ds(start, tile), s * LANES:(s + 1) * LANES], hi)
            vt = _act(v_ref[0, pl.ds(start, tile), s * LANES:(s + 1) * LANES], hi)
            for h in range(per_slab):
                i = s * per_slab + h
                outs.append(_sb_tile(qs[i], kt, vt, rs[i], u, mask, hi))
        return [o for o, _ in outs], [r for _, r in outs]

    def r_max(rs):
        m = jnp.max(rs[0])
        for r in rs[1:]:
            m = jnp.maximum(m, jnp.max(r))
        return m

    accs, rs = tiles(qi, [jnp.zeros((tile, 1), F32)] * len(qs), diag_mask)

    def cond(c):
        jj, _, _, m = c
        return (jj < qi) & (m > EXP_UNDERFLOW)

    def body(c):
        jj, accs, rs, _ = c
        outs, rs = tiles(qi - 1 - jj, rs, None)
        return jj + 1, [a + o for a, o in zip(accs, outs)], rs, r_max(rs)

    _, accs, rs, _ = lax.while_loop(cond, body, (jnp.int32(0), accs, rs, r_max(rs)))
    for s in range(n_slab):
        o_ref[0, :, s * LANES:(s + 1) * LANES] = jnp.where(
            lane < SB_HD, accs[s * per_slab], accs[s * per_slab + 1]).astype(o_ref.dtype)


def _upper_ones(n):
    j = lax.broadcasted_iota(jnp.int32, (n, n), 0)
    s = lax.broadcasted_iota(jnp.int32, (n, n), 1)
    return (j > s).astype(BF16)


def _attn_prompt(q, k, v, tile, hi, width=2 * LANES):
    b, t, sw = q.shape
    return pl.pallas_call(
        functools.partial(_attn_prompt_kernel, tile=tile, hi=hi),
        grid=(b, sw // width, t // tile),
        in_specs=[pl.BlockSpec((1, tile, width), lambda i, hp, qi: (i, qi, hp)),
                  pl.BlockSpec((1, t, width), lambda i, hp, qi: (i, 0, hp)),
                  pl.BlockSpec((1, t, width), lambda i, hp, qi: (i, 0, hp)),
                  _const_spec((tile, tile))],
        out_specs=pl.BlockSpec((1, tile, width), lambda i, hp, qi: (i, qi, hp)),
        out_shape=jax.ShapeDtypeStruct((b, t, sw), F32 if hi else BF16),
        compiler_params=_cparams(3, VMEM_MID),
        name="sb_attn_prompt",
    )(q, k, v, _upper_ones(tile))


def _attn_sample_kernel(q_ref, kn_ref, vn_ref, kc_ref, vc_ref, u_ref, o_ref, *, tile, hi):
    tq, sw = q_ref.shape[1], q_ref.shape[2]
    heads = sw // SB_HD
    past = kc_ref.shape[1]
    rows = heads * tq
    q = q_ref[0]
    lane = lax.broadcasted_iota(jnp.int32, (rows, sw), 1)
    row = lax.broadcasted_iota(jnp.int32, (rows, sw), 0)
    keep = lane // SB_HD == row // tq
    qbd = _act(jnp.where(keep, jnp.concatenate([q] * heads, axis=0), jnp.zeros((), q.dtype)), hi)
    u = u_ref[...]

    zpad = jnp.zeros((LANES - tq, sw), kn_ref.dtype)
    kn = _act(jnp.concatenate([kn_ref[0], zpad], axis=0), hi)
    vn = _act(jnp.concatenate([vn_ref[0], zpad], axis=0), hi)
    r_new = lax.broadcasted_iota(jnp.int32, (rows, LANES), 0)
    c_new = lax.broadcasted_iota(jnp.int32, (rows, LANES), 1)
    new_mask = c_new < (r_new % tq)
    acc, r = _sb_tile(qbd, kn, vn, jnp.zeros((rows, 1), F32), u[:LANES, :LANES], new_mask, hi)
    for j in range(past // tile - 1, -1, -1):
        kt = _act(kc_ref[0, j * tile:(j + 1) * tile, :], hi)
        vt = _act(vc_ref[0, j * tile:(j + 1) * tile, :], hi)
        o, r = _sb_tile(qbd, kt, vt, r, u, None, hi)
        acc = acc + o
    acc = jnp.where(keep, acc, 0.0)
    out = acc[0:tq]
    for h in range(1, heads):
        out = out + acc[h * tq:(h + 1) * tq]
    o_ref[0] = out.astype(o_ref.dtype)


def _attn_sample(qb, kb, vb, cache_k, cache_v, tile, hi):
    b, tq, sw = qb.shape
    past = cache_k.shape[1]
    per_b = lambda i: (i, 0, 0)
    return pl.pallas_call(
        functools.partial(_attn_sample_kernel, tile=tile, hi=hi),
        grid=(b,),
        in_specs=[pl.BlockSpec((1, tq, sw), per_b), pl.BlockSpec((1, tq, sw), per_b),
                  pl.BlockSpec((1, tq, sw), per_b),
                  pl.BlockSpec((1, past, sw), per_b), pl.BlockSpec((1, past, sw), per_b),
                  _const_spec((tile, tile))],
        out_specs=pl.BlockSpec((1, tq, sw), per_b),
        out_shape=jax.ShapeDtypeStruct((b, tq, sw), F32 if hi else BF16),
        compiler_params=_cparams(1, VMEM_MID),
        name="sb_attn_sample",
    )(qb, kb, vb, cache_k, cache_v, _upper_ones(tile))


def _swiglu_rows(act, wg, wu, wd, hi):
    ff = wg[0].shape[-1]
    acc = None
    for c0 in range(0, ff, FF_CHUNK):
        cs = slice(c0, c0 + FF_CHUNK)
        g = _mm(act, wg, cols=cs)
        up = _mm(act, wu, cols=cs)
        part = _mm(_act(_silu(g) * up, hi), wd, rows=cs)
        acc = part if acc is None else acc + part
    return acc


def _out_ffn_kernel(x_ref, yl_ref, oa_ref, mod_ref, gmp_ref, gfp_ref, gfo_ref, *refs, hi):
    wo, refs = _take_w(refs, hi)
    wg, refs = _take_w(refs, hi)
    wu, refs = _take_w(refs, hi)
    wd, refs = _take_w(refs, hi)
    out_ref, = refs
    nb, tt, d = x_ref.shape
    lw = yl_ref.shape[2]
    m = mod_ref[...]
    yl = _act(yl_ref[...].reshape(nb * tt, lw), hi)
    oa = _act(oa_ref[...].reshape(nb * tt, oa_ref.shape[2]), hi)
    y = _mm(yl, wo, rows=slice(0, lw)) + _mm(oa, wo, rows=slice(lw, None))
    x1 = x_ref[...] + m[:, 2:3, :] * _rms(y.reshape(nb, tt, d), gmp_ref[...])
    h = _rms(x1, gfp_ref[...]) * (1.0 + m[:, 4:5, :]) + m[:, 3:4, :]
    f = _swiglu_rows(_act(h.reshape(nb * tt, d), hi), wg, wu, wd, hi)
    out_ref[...] = x1 + m[:, 5:6, :] * _rms(f.reshape(nb, tt, d), gfo_ref[...])


def _out_ffn(x, yl, oa, mod, g_mix_post, g_ffn_pre, g_ffn_post, wo, wg, wu, wd, nb, tt, hi):
    b, t, d = x.shape
    row = lambda i, j: (i, j, 0)
    weights = [wo, wg, wu, wd]
    return pl.pallas_call(
        functools.partial(_out_ffn_kernel, hi=hi),
        grid=(b // nb, t // tt),
        in_specs=[pl.BlockSpec((nb, tt, d), row), pl.BlockSpec((nb, tt, yl.shape[2]), row),
                  pl.BlockSpec((nb, tt, oa.shape[2]), row),
                  pl.BlockSpec((nb, 6, d), lambda i, j: (i, 0, 0)),
                  _const_spec((1, d)), _const_spec((1, d)), _const_spec((1, d))]
        + [s for w in weights for s in _w_specs(w, hi)],
        out_specs=pl.BlockSpec((nb, tt, d), row),
        out_shape=jax.ShapeDtypeStruct((b, t, d), F32),
        compiler_params=_cparams(2, VMEM_BIG),
        name="out_proj_ffn",
    )(x, yl, oa, mod, g_mix_post, g_ffn_pre, g_ffn_post, *[a for w in weights for a in _w_args(w, hi)])


def _pool_router_kernel(x_ref, mod_ref, gpre_ref, pbuf_ref, ps_ref, gmp_ref, gfp_ref,
                        rwh_ref, rwl_ref, rb_ref, ltri_ref, *refs, pos0, n_experts, hi):
    pw, refs = _take_w(refs, hi)
    x1_ref, h2_ref, meta_ref, pout_ref, cnt_ref, hp_s, cnt_s = refs
    tt, d = x_ref.shape[1], x_ref.shape[2]
    hal = POOL_BUF + 1
    bi, ti = pl.program_id(0), pl.program_id(1)

    @pl.when((bi == 0) & (ti == 0))
    def _():
        cnt_s[...] = jnp.zeros_like(cnt_s)

    @pl.when(ti == 0)
    def _():
        hp_s[0:1, :] = jnp.zeros((1, d), F32)
        hp_s[1:hal, :] = pbuf_ref[0]

    m = mod_ref[0]
    x = x_ref[0]
    h = _rms(x, gpre_ref[...]) * (1.0 + m[1:2, :]) + m[0:1, :]
    hp_s[hal:hal + tt, :] = h

    pos = pos0 + ti * tt + lax.broadcasted_iota(jnp.int32, (tt, 1), 0)
    gd = d // len(POOL_WINDOWS)
    ys = []
    for g, w in enumerate(POOL_WINDOWS):
        lo = g * gd
        s = hp_s[hal:hal + tt, lo:lo + gd]
        for back in range(1, w):
            s = s + hp_s[hal - back:hal - back + tt, lo:lo + gd]
        cnt = jnp.minimum(w, pos + 1).astype(F32)
        pooled = s / cnt - h[:, lo:lo + gd]
        ys.append(_mm(_act(pooled, hi), (pw[0].at[g], None if pw[1] is None else pw[1].at[g])))
    y = jnp.concatenate(ys, axis=1) * ps_ref[...]
    tail = hp_s[hal + tt - POOL_BUF:hal + tt, :]
    hp_s[1:hal, :] = tail
    pout_ref[0] = tail

    x1 = x + m[2:3, :] * _rms(y, gmp_ref[...])
    x1_ref[0] = x1
    h2 = _rms(x1, gfp_ref[...]) * (1.0 + m[4:5, :]) + m[3:4, :]
    _rows_to_tiles(h2_ref.at[0], h2)

    logits = _mm(_split(h2), (rwh_ref, rwl_ref)) + rb_ref[...]
    lane = lax.broadcasted_iota(jnp.int32, (tt, LANES), 1)
    neg = jnp.float32(-jnp.inf)
    l1 = jnp.where(lane < n_experts, logits, neg)
    m1 = jnp.max(l1, axis=1, keepdims=True)
    i1 = jnp.min(jnp.where(l1 == m1, lane, LANES), axis=1, keepdims=True)
    l2 = jnp.where(lane == i1, neg, l1)
    m2 = jnp.max(l2, axis=1, keepdims=True)
    i2 = jnp.min(jnp.where(l2 == m2, lane, LANES), axis=1, keepdims=True)
    e21 = jnp.exp(m2 - m1)
    g1 = 1.0 / (1.0 + e21)
    g2 = e21 * g1
    sel = ((lane == i1) | (lane == i2)).astype(F32)
    rank = _dot(ltri_ref[...], sel.astype(BF16)) + cnt_s[...]
    r1 = jnp.sum(jnp.where(lane == i1, rank, 0.0), axis=1, keepdims=True)
    r2 = jnp.sum(jnp.where(lane == i2, rank, 0.0), axis=1, keepdims=True)
    cnt_new = cnt_s[...] + jnp.sum(sel, axis=0, keepdims=True)
    cnt_s[...] = cnt_new
    cnt_ref[...] = jnp.broadcast_to(cnt_new, cnt_ref.shape)
    meta = jnp.where(lane == 0, i1.astype(F32), 0.0)
    meta = jnp.where(lane == 1, i2.astype(F32), meta)
    meta = jnp.where(lane == 2, g1, meta)
    meta = jnp.where(lane == 3, g2, meta)
    meta = jnp.where(lane == 4, r1, meta)
    meta = jnp.where(lane == 5, r2, meta)
    meta_ref[0] = meta


def _lower_ones(n):
    i = lax.broadcasted_iota(jnp.int32, (n, n), 0)
    j = lax.broadcasted_iota(jnp.int32, (n, n), 1)
    return (j < i).astype(BF16)


def _pool_router(x, mod, g_pre, pbuf, pool_w, pool_scale, g_mix_post, g_ffn_pre, rw_hi, rw_lo, rb, tt, pos0,
                 n_experts, hi):
    b, t, d = x.shape
    row = lambda i, j: (i, j, 0)
    per_b = lambda i, j: (i, 0, 0)
    return pl.pallas_call(
        functools.partial(_pool_router_kernel, pos0=pos0, n_experts=n_experts, hi=hi),
        grid=(b, t // tt),
        in_specs=[pl.BlockSpec((1, tt, d), row), pl.BlockSpec((1, 6, d), per_b), _const_spec((1, d)),
                  pl.BlockSpec((1, POOL_BUF, d), per_b), _const_spec((1, d)),
                  _const_spec((1, d)), _const_spec((1, d)),
                  _const_spec(rw_hi.shape), _const_spec(rw_lo.shape), _const_spec((1, LANES)),
                  _const_spec((tt, tt))] + _w_specs(pool_w, hi, False),
        out_specs=[pl.BlockSpec((1, tt, d), row),
                   pl.BlockSpec((1, tt, SUBLANES, LANES), lambda i, j: (i, j, 0, 0)),
                   pl.BlockSpec((1, tt, LANES), row), pl.BlockSpec((1, POOL_BUF, d), per_b),
                   _const_spec((8, LANES))],
        out_shape=[jax.ShapeDtypeStruct((b, t, d), F32), jax.ShapeDtypeStruct((b, t, SUBLANES, LANES), F32),
                   jax.ShapeDtypeStruct((b, t, LANES), F32), jax.ShapeDtypeStruct((b, POOL_BUF, d), F32),
                   jax.ShapeDtypeStruct((8, LANES), F32)],
        scratch_shapes=[pltpu.VMEM((tt + POOL_BUF + 1, d), F32), pltpu.VMEM((1, LANES), F32)],
        compiler_params=_cparams(2, VMEM_MID),
        name="pool_router",
    )(x, mod, g_pre, pbuf, pool_scale, g_mix_post, g_ffn_pre, rw_hi, rw_lo, rb, _lower_ones(tt),
      *_w_args(pool_w, hi))


def _gather_kernel(idx_ref, src_ref, o_ref, sem, *, tg):
    base = pl.program_id(0) * tg

    def row_copy(r, t):
        return pltpu.make_async_copy(src_ref.at[t], o_ref.at[r], sem)

    def issue(r, c):
        row_copy(r, idx_ref[base + r]).start()
        return c

    def drain(r, c):
        row_copy(r, 0).wait()
        return c

    lax.fori_loop(0, tg, issue, 0, unroll=8)
    lax.fori_loop(0, tg, drain, 0, unroll=8)


def _gather_rows(src, idx, tg):
    n = idx.shape[0]
    tile = src.shape[1:]
    return pl.pallas_call(
        functools.partial(_gather_kernel, tg=tg),
        grid_spec=pltpu.PrefetchScalarGridSpec(
            num_scalar_prefetch=1,
            grid=(n // tg,),
            in_specs=[pl.BlockSpec(memory_space=pl.ANY)],
            out_specs=pl.BlockSpec((tg,) + tile, lambda i, idx_ref: (i, 0, 0)),
            scratch_shapes=[pltpu.SemaphoreType.DMA(())]),
        out_shape=jax.ShapeDtypeStruct((n,) + tile, src.dtype),
        compiler_params=_cparams(1),
        name="gather_rows",
    )(idx, src)


def _moe_kernel(te_ref, na_ref, xs_ref, wg_ref, wu_ref, wd_ref, ys_ref):
    active = pl.program_id(0) < na_ref[0]

    @pl.when(active)
    def _():
        y = _swiglu_rows(_act(_rows_from_tiles(xs_ref), False), (wg_ref, None), (wu_ref, None), (wd_ref, None),
                         False)
        _rows_to_tiles(ys_ref, y)

    @pl.when(jnp.logical_not(active))
    def _():
        ys_ref[...] = jnp.zeros_like(ys_ref)


def _moe_grouped(xs, tile_expert, n_active, wg_b, wu_b, wd_b, tm):
    p = xs.shape[0]
    d, ff = wg_b.shape[1], wg_b.shape[2]
    rows = lambda i, te, na: (jnp.minimum(i, na[0] - 1), 0, 0)
    expert = lambda i, te, na: (te[i], 0, 0)
    return pl.pallas_call(
        _moe_kernel,
        grid_spec=pltpu.PrefetchScalarGridSpec(
            num_scalar_prefetch=2,
            grid=(p // tm,),
            in_specs=[pl.BlockSpec((tm, SUBLANES, LANES), rows),
                      pl.BlockSpec((None, d, ff), expert), pl.BlockSpec((None, d, ff), expert),
                      pl.BlockSpec((None, ff, d), expert)],
            out_specs=pl.BlockSpec((tm, SUBLANES, LANES), lambda i, te, na: (i, 0, 0))),
        out_shape=jax.ShapeDtypeStruct((p, SUBLANES, LANES), F32),
        compiler_params=_cparams(1, VMEM_BIG),
        name="moe_grouped_swiglu",
    )(tile_expert, n_active, xs, wg_b, wu_b, wd_b)


def _combine_kernel(x1_ref, ya_ref, yb_ref, meta_ref, mod_ref, gfo_ref, out_ref):
    m = mod_ref[...]
    meta = meta_ref[...]
    y = meta[:, :, 2:3] * _rows_from_tiles(ya_ref) + meta[:, :, 3:4] * _rows_from_tiles(yb_ref)
    out_ref[...] = x1_ref[...] + m[:, 5:6, :] * _rms(y, gfo_ref[...])


def _combine(x1, ya, yb, meta, mod, g_ffn_post, nb, tt):
    b, t, d = x1.shape
    row = lambda i, j: (i, j, 0)
    tiles = pl.BlockSpec((nb, tt, SUBLANES, LANES), lambda i, j: (i, j, 0, 0))
    return pl.pallas_call(
        _combine_kernel,
        grid=(b // nb, t // tt),
        in_specs=[pl.BlockSpec((nb, tt, d), row), tiles, tiles,
                  pl.BlockSpec((nb, tt, LANES), row), pl.BlockSpec((nb, 6, d), lambda i, j: (i, 0, 0)),
                  _const_spec((1, d))],
        out_specs=pl.BlockSpec((nb, tt, d), row),
        out_shape=jax.ShapeDtypeStruct((b, t, d), F32),
        compiler_params=_cparams(2, VMEM_MID),
        name="moe_combine",
    )(x1, ya, yb, meta, mod, g_ffn_post)


def _block_diag(w):
    heads, hd, _ = w.shape
    eye = jnp.eye(heads, dtype=w.dtype)
    return jnp.einsum("hij,hg->higj", w, eye).reshape(heads * hd, heads * hd)


def _hi_lo(w):
    hi = w.astype(BF16)
    return hi, (w - hi.astype(F32)).astype(BF16)


def _moe_layer(x1, h2, meta, counts, mod, g_ffn_post, wg_b, wu_b, wd_b, tm, nb, tt):
    b, t, d = x1.shape
    n = b * t
    n_experts = wg_b.shape[0]
    n_tiles = (TOP_K * n) // tm + n_experts
    meta2 = meta.reshape(n, LANES)
    e_idx = meta2[:, 0:2].astype(jnp.int32)
    rank = meta2[:, 4:6].astype(jnp.int32)
    cnt = counts[0, :n_experts].astype(jnp.int32)
    tiles_e = (cnt + tm - 1) // tm
    tiles_end = jnp.cumsum(tiles_e)
    group_off = (tiles_end - tiles_e) * tm
    n_active = tiles_end[-1:]
    tile_id = jnp.minimum(jnp.arange(n_tiles, dtype=jnp.int32), n_active[0] - 1)
    tile_expert = jnp.minimum(jnp.sum((tile_id[:, None] >= tiles_end[None, :]).astype(jnp.int32), axis=1),
                              n_experts - 1)
    pos = group_off[e_idx] + rank
    tok = jnp.broadcast_to(jnp.arange(n, dtype=jnp.int32)[:, None], (n, TOP_K))
    slot_tok = jnp.zeros((n_tiles * tm,), jnp.int32).at[pos.reshape(-1)].set(tok.reshape(-1))

    tg = min(512, n_tiles * tm)
    xs = _gather_rows(h2.reshape(n, SUBLANES, LANES), slot_tok, tg)
    ys = _moe_grouped(xs, tile_expert, n_active.astype(jnp.int32), wg_b, wu_b, wd_b, tm)
    tg2 = min(512, n)
    ya = _gather_rows(ys, pos[:, 0], tg2).reshape(b, t, SUBLANES, LANES)
    yb = _gather_rows(ys, pos[:, 1], tg2).reshape(b, t, SUBLANES, LANES)
    return _combine(x1, ya, yb, meta, mod, g_ffn_post, nb, tt)


def _trunk(x, mod_all, pos0, cache_k, cache_v, st_h, st_conv, st_pool, p, cfg):
    b, t, d = x.shape
    depth = mod_all.shape[0]
    nb, tt = cfg["nb"], cfg["tt"]
    lw = p["conv_w"].shape[2]
    sw = (p["w_in"][0][0].shape[1] - 2 * lw) // 3
    heads = sw // SB_HD
    n_experts = p["moe_wg_b"].shape[1]
    row1 = lambda a: a.reshape(1, -1)
    new_k, new_v, new_h, new_conv, new_pool = [], [], [], [], []
    for l in range(depth):
        e = l // 2
        hi = l in cfg["hi_layers"]
        mod = mod_all[l].reshape(b, 6, d)
        if l % 2 == 0:
            proj = _in_proj(x, mod, row1(p["g_mix_pre"][l]), p["w_in"][e], nb, tt, lw, sw, hi)
            gate, lru_in, q, k32, v32 = proj[:5]
            if cache_k is None:
                cbuf = jnp.zeros((b, CONV_W - 1, lw), F32)
                h0 = jnp.zeros((b, lw), F32)
            else:
                cbuf, h0 = st_conv[e], st_h[e]
            yl, h_last, cb = _lru(lru_in, gate, cbuf, h0, p["conv_w"][e], p["conv_b"][e], p["wa"][e],
                                  p["lru_ba"][e], p["wx"][e], p["lru_bx"][e], p["lru_lambda"][e], cfg["lru_tt"], hi)
            kk, vv = (k32, v32) if hi else proj[5:7]
            if cache_k is None:
                oa = _attn_prompt(q, kk, vv, cfg["attn_tile"], hi)
            else:
                past = cache_k.shape[2]
                oa = _attn_sample(q, kk, vv, cache_k[e].reshape(b, past, sw),
                                  cache_v[e].reshape(b, past, sw), cfg["attn_tile"], hi)
            x = _out_ffn(x, yl, oa, mod, row1(p["g_mix_post"][l]), row1(p["g_ffn_pre"][l]),
                         row1(p["g_ffn_post"][l]), p["w_out"][e], p["ffn_wg"][e], p["ffn_wu"][e],
                         p["ffn_wd"][e], nb, cfg["hi_tt"] if hi else tt, hi)
            new_k.append(k32.reshape(b, t, heads, SB_HD))
            new_v.append(v32.reshape(b, t, heads, SB_HD))
            new_h.append(h_last.reshape(b, lw))
            new_conv.append(cb)
        else:
            pbuf = jnp.zeros((b, POOL_BUF, d), F32) if st_pool is None else st_pool[e]
            x1, h2, meta, pb, counts = _pool_router(
                x, mod, row1(p["g_mix_pre"][l]), pbuf, p["pool_w"][e], row1(p["pool_scale"][e]),
                row1(p["g_mix_post"][l]), row1(p["g_ffn_pre"][l]), p["rw_hi"][e], p["rw_lo"][e], p["rb"][e],
                cfg["pool_tt"], pos0, n_experts, hi)
            x = _moe_layer(x1, h2, meta, counts, mod, row1(p["g_ffn_post"][l]), p["moe_wg_b"][e], p["moe_wu_b"][e],
                           p["moe_wd_b"][e], cfg["moe_tm"], nb, tt)
            new_pool.append(pb)
    return (x, jnp.stack(new_k), jnp.stack(new_v), jnp.stack(new_h), jnp.stack(new_conv), jnp.stack(new_pool))


def kernel(x_prompt, x_sample, cache_sb_k, cache_sb_v, state_lru_h, state_lru_conv, state_pool, c_prompt, c_sample, ada_w, ada_b, g_mix_pre, g_mix_post, g_ffn_pre, g_ffn_post, w_in, w_out, conv_w, conv_b, lru_wa, lru_ba, lru_wx, lru_bx, lru_lambda, pool_w, pool_scale, ffn_w_gate, ffn_w_up, ffn_w_down, router_w, router_b, moe_w_gate, moe_w_up, moe_w_down):
    bp = x_prompt.shape[0]
    n_experts = router_w.shape[2]
    rw_hi, rw_lo = _hi_lo(jnp.pad(router_w, ((0, 0), (0, 0), (0, LANES - n_experts))))

    def per_layer(w):
        return [_hi_lo(w[e]) if e == 0 else (w[e].astype(BF16), None) for e in range(w.shape[0])]

    p = dict(
        g_mix_pre=g_mix_pre, g_mix_post=g_mix_post, g_ffn_pre=g_ffn_pre, g_ffn_post=g_ffn_post,
        w_in=per_layer(w_in), w_out=per_layer(w_out),
        conv_w=conv_w, conv_b=conv_b, lru_ba=lru_ba, lru_bx=lru_bx, lru_lambda=lru_lambda,
        wa=per_layer(jax.vmap(_block_diag)(lru_wa)), wx=per_layer(jax.vmap(_block_diag)(lru_wx)),
        pool_w=per_layer(pool_w), pool_scale=pool_scale,
        ffn_wg=per_layer(ffn_w_gate), ffn_wu=per_layer(ffn_w_up), ffn_wd=per_layer(ffn_w_down),
        rw_hi=rw_hi, rw_lo=rw_lo,
        rb=jnp.pad(router_b, ((0, 0), (0, LANES - n_experts))).reshape(-1, 1, LANES),
        moe_wg_b=moe_w_gate.astype(BF16), moe_wu_b=moe_w_up.astype(BF16), moe_wd_b=moe_w_down.astype(BF16),
    )
    mod_all = _ada_mod(jnp.concatenate([c_prompt, c_sample], axis=0), ada_w, ada_b)
    cfg_prompt = dict(nb=1, tt=512, hi_tt=256, lru_tt=512, pool_tt=512, attn_tile=256, moe_tm=512, hi_layers=(0, 1))
    ts = x_sample.shape[1]
    cfg_sample = dict(nb=x_sample.shape[0] // 2, tt=ts, hi_tt=ts, lru_tt=ts, pool_tt=ts, attn_tile=256, moe_tm=128,
                      hi_layers=(0, 1))
    yp, kp, vp, hp, cp, pp = _trunk(x_prompt, mod_all[:, :bp], 0, None, None, None, None, None, p, cfg_prompt)
    ys, ks, vs, hs, cs, ps = _trunk(x_sample, mod_all[:, bp:], cache_sb_k.shape[2], cache_sb_k, cache_sb_v,
                                    state_lru_h, state_lru_conv, state_pool, p, cfg_sample)
    return (yp, ys, kp, vp, hp, cp, pp, ks, vs, hs, cs, ps)
```

```python
import functools

import jax
import jax.numpy as jnp
from jax import lax
from jax.experimental import pallas as pl
from jax.experimental.pallas import tpu as pltpu

F32 = jnp.float32
BF16 = jnp.bfloat16

EPS = 1e-6
LRU_C = 8.0
CONV_W = 4
SB_HD = 64
POOL_WINDOWS = (2, 4, 8, 16)
POOL_BUF = max(POOL_WINDOWS) - 1
TOP_K = 2
LANES = 128
SUBLANES = 8
FF_CHUNK = 256
EXP_UNDERFLOW = -104.0

VMEM_BIG = 56 * 1024 * 1024
VMEM_MID = 40 * 1024 * 1024


def _cparams(n_axes, vmem=None):
    return pltpu.CompilerParams(dimension_semantics=("arbitrary",) * n_axes, vmem_limit_bytes=vmem)


def _rms(x, g):
    return x * lax.rsqrt(jnp.mean(x * x, axis=-1, keepdims=True) + EPS) * g


def _silu(x):
    return x * jax.nn.sigmoid(x)


def _gelu_tanh(x):
    return 0.5 * x * (1.0 + jnp.tanh(0.7978845608028654 * (x + 0.044715 * (x * x * x))))


def _dot(a, b):
    return jnp.dot(a, b, preferred_element_type=F32)


def _dot_nt(a, b):
    return lax.dot_general(a, b, (((1,), (1,)), ((), ())), preferred_element_type=F32)


def _const_spec(shape, single=False):
    nd = len(shape)
    if single:
        return pl.BlockSpec(shape, lambda *_: (0,) * nd, pipeline_mode=pl.Buffered(1))
    return pl.BlockSpec(shape, lambda *_: (0,) * nd)


def _split(a):
    hi = a.astype(BF16)
    return hi, (a - hi.astype(F32)).astype(BF16)


def _act(a, hi):
    return _split(a) if hi else (a.astype(BF16), None)


def _mm(act, w, rows=slice(None), cols=slice(None)):
    ah, al = act
    wh = w[0][rows, cols]
    out = _dot(ah, wh)
    if al is not None:
        out = out + (_dot(al, wh) + _dot(ah, w[1][rows, cols]))
    return out


def _take_w(refs, hi):
    if hi:
        return (refs[0], refs[1]), refs[2:]
    return (refs[0], None), refs[1:]


def _w_args(w, hi):
    return list(w) if hi else [w[0]]


def _w_specs(w, hi, single=True):
    return [_const_spec(a.shape, single) for a in _w_args(w, hi)]


def _rows_from_tiles(ref):
    return jnp.concatenate([ref[..., s, :] for s in range(SUBLANES)], axis=-1)


def _rows_to_tiles(ref, val):
    for s in range(SUBLANES):
        ref[..., s, :] = val[..., s * LANES:(s + 1) * LANES]


def _ada_kernel(c_ref, w_ref, b_ref, o_ref):
    ch, cl = _split(_silu(c_ref[...]))
    wh, wl = _split(w_ref[...])
    o_ref[...] = _dot(ch, wh) + (_dot(cl, wh) + _dot(ch, wl)) + b_ref[...]


def _ada_mod(c_all, ada_w, ada_b, tn=1536):
    depth, d, n6 = ada_w.shape
    r = c_all.shape[0]
    return pl.pallas_call(
        _ada_kernel,
        grid=(depth, n6 // tn),
        in_specs=[pl.BlockSpec((r, d), lambda l, j: (0, 0)),
                  pl.BlockSpec((None, d, tn), lambda l, j: (l, 0, j)),
                  pl.BlockSpec((None, 1, tn), lambda l, j: (l, 0, j))],
        out_specs=pl.BlockSpec((None, r, tn), lambda l, j: (l, 0, j)),
        out_shape=jax.ShapeDtypeStruct((depth, r, n6), F32),
        compiler_params=_cparams(2, VMEM_MID),
        name="ada_mod",
    )(c_all, ada_w, ada_b.reshape(depth, 1, n6))


def _in_proj_kernel(x_ref, mod_ref, g_ref, *refs, lw, sw, hi):
    w, outs = _take_w(refs, hi)
    nb, tt, d = x_ref.shape
    m = mod_ref[...]
    h = _rms(x_ref[...], g_ref[...]) * (1.0 + m[:, 1:2, :]) + m[:, 0:1, :]
    act = _act(h.reshape(nb * tt, d), hi)

    def proj(c0, c1):
        return _mm(act, w, cols=slice(c0, c1)).reshape(nb, tt, c1 - c0)

    outs[0][...] = proj(0, lw)
    outs[1][...] = proj(lw, 2 * lw)
    outs[2][...] = (proj(2 * lw, 2 * lw + sw) * (SB_HD ** -0.5)).astype(outs[2].dtype)
    k = proj(2 * lw + sw, 2 * lw + 2 * sw)
    v = proj(2 * lw + 2 * sw, 2 * lw + 3 * sw)
    outs[3][...] = k
    outs[4][...] = v
    if not hi:
        outs[5][...] = k.astype(BF16)
        outs[6][...] = v.astype(BF16)


def _in_proj(x, mod, g, w, nb, tt, lw, sw, hi):
    b, t, d = x.shape
    row = lambda i, j: (i, j, 0)
    spec = lambda n: pl.BlockSpec((nb, tt, n), row)
    sds = lambda n, dt: jax.ShapeDtypeStruct((b, t, n), dt)
    out_shape = [sds(lw, F32), sds(lw, F32), sds(sw, F32 if hi else BF16), sds(sw, F32), sds(sw, F32)]
    out_specs = [spec(lw), spec(lw), spec(sw), spec(sw), spec(sw)]
    if not hi:
        out_shape += [sds(sw, BF16), sds(sw, BF16)]
        out_specs += [spec(sw), spec(sw)]
    return pl.pallas_call(
        functools.partial(_in_proj_kernel, lw=lw, sw=sw, hi=hi),
        grid=(b // nb, t // tt),
        in_specs=[spec(d), pl.BlockSpec((nb, 6, d), lambda i, j: (i, 0, 0)), _const_spec((1, d))]
        + _w_specs(w, hi),
        out_specs=out_specs,
        out_shape=out_shape,
        compiler_params=_cparams(2, VMEM_MID),
        name="in_proj",
    )(x, mod, g, *_w_args(w, hi))


def _lru_kernel(xin_ref, gate_ref, cbuf_ref, h0_ref, cw_ref, cb_ref, ba_ref, bx_ref, lam_ref, *refs, hi):
    wa, refs = _take_w(refs, hi)
    wx, refs = _take_w(refs, hi)
    y_ref, hlast_ref, cout_ref, xp_s, a_s, u_s, hs_s, hcar_s = refs
    tt = xin_ref.shape[1]
    pad = 8

    @pl.when(pl.program_id(1) == 0)
    def _():
        xp_s[pad - (CONV_W - 1):pad, :] = cbuf_ref[0]
        hcar_s[...] = h0_ref[0]

    x = xin_ref[0]
    xp_s[pad:pad + tt, :] = x
    w = cw_ref[...]
    xc = cb_ref[...] + xp_s[pad - 3:pad - 3 + tt, :] * w[0:1]
    xc = xc + xp_s[pad - 2:pad - 2 + tt, :] * w[1:2]
    xc = xc + xp_s[pad - 1:pad - 1 + tt, :] * w[2:3]
    xc = xc + x * w[3:4]
    tail = xp_s[pad + tt - (CONV_W - 1):pad + tt, :]
    xp_s[pad - (CONV_W - 1):pad, :] = tail
    cout_ref[0] = tail

    act = _act(xc, hi)
    r = jax.nn.sigmoid(_mm(act, wa) + ba_ref[...])
    i = jax.nn.sigmoid(_mm(act, wx) + bx_ref[...])
    nl = -lam_ref[...]
    softplus = jnp.maximum(nl, 0.0) + jnp.log1p(jnp.exp(-jnp.abs(nl)))
    log_a = (-LRU_C) * r * softplus
    a = jnp.exp(log_a)
    a_s[...] = a
    u_s[...] = jnp.sqrt(jnp.tanh(-log_a) * (a * a + 1.0)) * (i * xc)

    def step(s, h):
        h = a_s[pl.ds(s, 1), :] * h + u_s[pl.ds(s, 1), :]
        hs_s[pl.ds(s, 1), :] = h
        return h

    h = lax.fori_loop(0, tt, step, hcar_s[...], unroll=8)
    hcar_s[...] = h
    hlast_ref[0] = h
    y_ref[0] = (hs_s[...] * _gelu_tanh(gate_ref[0])).astype(y_ref.dtype)


def _lru(lru_in, gate, cbuf, h0, conv_w, conv_b, wa, ba, wx, bx, lam, tt, hi):
    b, t, c = lru_in.shape
    row = lambda i, j: (i, j, 0)
    per_b = lambda i, j: (i, 0, 0)
    vec = lambda a: a.reshape(1, c)
    return pl.pallas_call(
        functools.partial(_lru_kernel, hi=hi),
        grid=(b, t // tt),
        in_specs=[pl.BlockSpec((1, tt, c), row), pl.BlockSpec((1, tt, c), row),
                  pl.BlockSpec((1, CONV_W - 1, c), per_b), pl.BlockSpec((1, 1, c), per_b),
                  _const_spec((CONV_W, c)), _const_spec((1, c)), _const_spec((1, c)), _const_spec((1, c)),
                  _const_spec((1, c))] + _w_specs(wa, hi, False) + _w_specs(wx, hi, False),
        out_specs=[pl.BlockSpec((1, tt, c), row), pl.BlockSpec((1, 1, c), per_b),
                   pl.BlockSpec((1, CONV_W - 1, c), per_b)],
        out_shape=[jax.ShapeDtypeStruct((b, t, c), F32 if hi else BF16), jax.ShapeDtypeStruct((b, 1, c), F32),
                   jax.ShapeDtypeStruct((b, CONV_W - 1, c), F32)],
        scratch_shapes=[pltpu.VMEM((tt + 8, c), F32), pltpu.VMEM((tt, c), F32), pltpu.VMEM((tt, c), F32),
                        pltpu.VMEM((tt, c), F32), pltpu.VMEM((1, c), F32)],
        compiler_params=_cparams(2, VMEM_MID),
        name="rg_lru",
    )(lru_in, gate, cbuf, h0.reshape(b, 1, c), conv_w, vec(conv_b), vec(ba), vec(bx), vec(lam),
      *_w_args(wa, hi), *_w_args(wx, hi))


def _sb_tile(q, k, v, r_in, u, mask, hi):
    z = _dot_nt(q[0], k[0])
    if hi:
        z = z + (_dot_nt(q[1], k[0]) + _dot_nt(q[0], k[1]))
    soft = jnp.log(1.0 + jnp.exp(-jnp.abs(z)))
    log_om = jnp.minimum(-z, 0.0) - soft
    log_beta = log_om + z
    if mask is not None:
        log_om = jnp.where(mask, log_om, 0.0)
    lh, ll = _split(log_om)
    stick = _dot(lh, u) + _dot(ll, u) + r_in
    a = jnp.exp(log_beta + stick)
    if mask is not None:
        a = jnp.where(mask, a, 0.0)
    ah, al = _act(a, hi)
    o = _dot(ah, v[0])
    if hi:
        o = o + (_dot(al, v[0]) + _dot(ah, v[1]))
    return o, r_in + jnp.sum(log_om, axis=1, keepdims=True)


def _attn_prompt_kernel(q_ref, k_ref, v_ref, u_ref, o_ref, *, tile, hi):
    qi = pl.program_id(2)
    n_slab = q_ref.shape[2] // LANES
    per_slab = LANES // SB_HD
    u = u_ref[...]
    lane = lax.broadcasted_iota(jnp.int32, (tile, LANES), 1)
    row = lax.broadcasted_iota(jnp.int32, (tile, tile), 0)
    col = lax.broadcasted_iota(jnp.int32, (tile, tile), 1)
    diag_mask = col < row

    qs = []
    for s in range(n_slab):
        q = q_ref[0, :, s * LANES:(s + 1) * LANES]
        for h in range(per_slab):
            in_head = (lane >= h * SB_HD) & (lane < (h + 1) * SB_HD)
            qs.append(_act(jnp.where(in_head, q, jnp.zeros_like(q)), hi))

    def tiles(j, rs, mask):
        start = pl.multiple_of(j * tile, tile)
        outs = []
        for s in range(n_slab):
            kt = _act(k_ref[0, pl.ds(start, tile), s * LANES:(s + 1) * LANES], hi)
            vt = _act(v_ref[0, pl.ds(start, tile), s * LANES:(s + 1) * LANES], hi)
            for h in range(per_slab):
                i = s * per_slab + h
                outs.append(_sb_tile(qs[i], kt, vt, rs[i], u, mask, hi))
        return [o for o, _ in outs], [r for _, r in outs]

    def r_max(rs):
        m = jnp.max(rs[0])
        for r in rs[1:]:
            m = jnp.maximum(m, jnp.max(r))
        return m

    accs, rs = tiles(qi, [jnp.zeros((tile, 1), F32)] * len(qs), diag_mask)

    def cond(c):
        jj, _, _, m = c
        return (jj < qi) & (m > EXP_UNDERFLOW)

    def body(c):
        jj, accs, rs, _ = c
        outs, rs = tiles(qi - 1 - jj, rs, None)
        return jj + 1, [a + o for a, o in zip(accs, outs)], rs, r_max(rs)

    _, accs, rs, _ = lax.while_loop(cond, body, (jnp.int32(0), accs, rs, r_max(rs)))
    for s in range(n_slab):
        o_ref[0, :, s * LANES:(s + 1) * LANES] = jnp.where(
            lane < SB_HD, accs[s * per_slab], accs[s * per_slab + 1]).astype(o_ref.dtype)


def _upper_ones(n):
    j = lax.broadcasted_iota(jnp.int32, (n, n), 0)
    s = lax.broadcasted_iota(jnp.int32, (n, n), 1)
    return (j > s).astype(BF16)


def _attn_prompt(q, k, v, tile, hi, width=2 * LANES):
    b, t, sw = q.shape
    return pl.pallas_call(
        functools.partial(_attn_prompt_kernel, tile=tile, hi=hi),
        grid=(b, sw // width, t // tile),
        in_specs=[pl.BlockSpec((1, tile, width), lambda i, hp, qi: (i, qi, hp)),
                  pl.BlockSpec((1, t, width), lambda i, hp, qi: (i, 0, hp)),
                  pl.BlockSpec((1, t, width), lambda i, hp, qi: (i, 0, hp)),
                  _const_spec((tile, tile))],
        out_specs=pl.BlockSpec((1, tile, width), lambda i, hp, qi: (i, qi, hp)),
        out_shape=jax.ShapeDtypeStruct((b, t, sw), F32 if hi else BF16),
        compiler_params=_cparams(3, VMEM_MID),
        name="sb_attn_prompt",
    )(q, k, v, _upper_ones(tile))


def _attn_sample_kernel(q_ref, kn_ref, vn_ref, kc_ref, vc_ref, u_ref, o_ref, *, tile, hi):
    tq, sw = q_ref.shape[1], q_ref.shape[2]
    heads = sw // SB_HD
    past = kc_ref.shape[1]
    rows = heads * tq
    q = q_ref[0]
    lane = lax.broadcasted_iota(jnp.int32, (rows, sw), 1)
    row = lax.broadcasted_iota(jnp.int32, (rows, sw), 0)
    keep = lane // SB_HD == row // tq
    qbd = _act(jnp.where(keep, jnp.concatenate([q] * heads, axis=0), jnp.zeros((), q.dtype)), hi)
    u = u_ref[...]

    zpad = jnp.zeros((LANES - tq, sw), kn_ref.dtype)
    kn = _act(jnp.concatenate([kn_ref[0], zpad], axis=0), hi)
    vn = _act(jnp.concatenate([vn_ref[0], zpad], axis=0), hi)
    r_new = lax.broadcasted_iota(jnp.int32, (rows, LANES), 0)
    c_new = lax.broadcasted_iota(jnp.int32, (rows, LANES), 1)
    new_mask = c_new < (r_new % tq)
    acc, r = _sb_tile(qbd, kn, vn, jnp.zeros((rows, 1), F32), u[:LANES, :LANES], new_mask, hi)
    for j in range(past // tile - 1, -1, -1):
        kt = _act(kc_ref[0, j * tile:(j + 1) * tile, :], hi)
        vt = _act(vc_ref[0, j * tile:(j + 1) * tile, :], hi)
        o, r = _sb_tile(qbd, kt, vt, r, u, None, hi)
        acc = acc + o
    acc = jnp.where(keep, acc, 0.0)
    out = acc[0:tq]
    for h in range(1, heads):
        out = out + acc[h * tq:(h + 1) * tq]
    o_ref[0] = out.astype(o_ref.dtype)


def _attn_sample(qb, kb, vb, cache_k, cache_v, tile, hi):
    b, tq, sw = qb.shape
    past = cache_k.shape[1]
    per_b = lambda i: (i, 0, 0)
    return pl.pallas_call(
        functools.partial(_attn_sample_kernel, tile=tile, hi=hi),
        grid=(b,),
        in_specs=[pl.BlockSpec((1, tq, sw), per_b), pl.BlockSpec((1, tq, sw), per_b),
                  pl.BlockSpec((1, tq, sw), per_b),
                  pl.BlockSpec((1, past, sw), per_b), pl.BlockSpec((1, past, sw), per_b),
                  _const_spec((tile, tile))],
        out_specs=pl.BlockSpec((1, tq, sw), per_b),
        out_shape=jax.ShapeDtypeStruct((b, tq, sw), F32 if hi else BF16),
        compiler_params=_cparams(1, VMEM_MID),
        name="sb_attn_sample",
    )(qb, kb, vb, cache_k, cache_v, _upper_ones(tile))


def _swiglu_rows(act, wg, wu, wd, hi):
    ff = wg[0].shape[-1]
    acc = None
    for c0 in range(0, ff, FF_CHUNK):
        cs = slice(c0, c0 + FF_CHUNK)
        g = _mm(act, wg, cols=cs)
        up = _mm(act, wu, cols=cs)
        part = _mm(_act(_silu(g) * up, hi), wd, rows=cs)
        acc = part if acc is None else acc + part
    return acc


def _out_ffn_kernel(x_ref, yl_ref, oa_ref, mod_ref, gmp_ref, gfp_ref, gfo_ref, *refs, hi):
    wo, refs = _take_w(refs, hi)
    wg, refs = _take_w(refs, hi)
    wu, refs = _take_w(refs, hi)
    wd, refs = _take_w(refs, hi)
    out_ref, = refs
    nb, tt, d = x_ref.shape
    lw = yl_ref.shape[2]
    m = mod_ref[...]
    yl = _act(yl_ref[...].reshape(nb * tt, lw), hi)
    oa = _act(oa_ref[...].reshape(nb * tt, oa_ref.shape[2]), hi)
    y = _mm(yl, wo, rows=slice(0, lw)) + _mm(oa, wo, rows=slice(lw, None))
    x1 = x_ref[...] + m[:, 2:3, :] * _rms(y.reshape(nb, tt, d), gmp_ref[...])
    h = _rms(x1, gfp_ref[...]) * (1.0 + m[:, 4:5, :]) + m[:, 3:4, :]
    f = _swiglu_rows(_act(h.reshape(nb * tt, d), hi), wg, wu, wd, hi)
    out_ref[...] = x1 + m[:, 5:6, :] * _rms(f.reshape(nb, tt, d), gfo_ref[...])


def _out_ffn(x, yl, oa, mod, g_mix_post, g_ffn_pre, g_ffn_post, wo, wg, wu, wd, nb, tt, hi):
    b, t, d = x.shape
    row = lambda i, j: (i, j, 0)
    weights = [wo, wg, wu, wd]
    return pl.pallas_call(
        functools.partial(_out_ffn_kernel, hi=hi),
        grid=(b // nb, t // tt),
        in_specs=[pl.BlockSpec((nb, tt, d), row), pl.BlockSpec((nb, tt, yl.shape[2]), row),
                  pl.BlockSpec((nb, tt, oa.shape[2]), row),
                  pl.BlockSpec((nb, 6, d), lambda i, j: (i, 0, 0)),
                  _const_spec((1, d)), _const_spec((1, d)), _const_spec((1, d))]
        + [s for w in weights for s in _w_specs(w, hi)],
        out_specs=pl.BlockSpec((nb, tt, d), row),
        out_shape=jax.ShapeDtypeStruct((b, t, d), F32),
        compiler_params=_cparams(2, VMEM_BIG),
        name="out_proj_ffn",
    )(x, yl, oa, mod, g_mix_post, g_ffn_pre, g_ffn_post, *[a for w in weights for a in _w_args(w, hi)])


def _pool_router_kernel(x_ref, mod_ref, gpre_ref, pbuf_ref, ps_ref, gmp_ref, gfp_ref,
                        rwh_ref, rwl_ref, rb_ref, ltri_ref, *refs, pos0, n_experts, hi):
    pw, refs = _take_w(refs, hi)
    x1_ref, h2_ref, meta_ref, pout_ref, cnt_ref, hp_s, cnt_s = refs
    tt, d = x_ref.shape[1], x_ref.shape[2]
    hal = POOL_BUF + 1
    bi, ti = pl.program_id(0), pl.program_id(1)

    @pl.when((bi == 0) & (ti == 0))
    def _():
        cnt_s[...] = jnp.zeros_like(cnt_s)

    @pl.when(ti == 0)
    def _():
        hp_s[0:1, :] = jnp.zeros((1, d), F32)
        hp_s[1:hal, :] = pbuf_ref[0]

    m = mod_ref[0]
    x = x_ref[0]
    h = _rms(x, gpre_ref[...]) * (1.0 + m[1:2, :]) + m[0:1, :]
    hp_s[hal:hal + tt, :] = h

    pos = pos0 + ti * tt + lax.broadcasted_iota(jnp.int32, (tt, 1), 0)
    gd = d // len(POOL_WINDOWS)
    ys = []
    for g, w in enumerate(POOL_WINDOWS):
        lo = g * gd
        s = hp_s[hal:hal + tt, lo:lo + gd]
        for back in range(1, w):
            s = s + hp_s[hal - back:hal - back + tt, lo:lo + gd]
        cnt = jnp.minimum(w, pos + 1).astype(F32)
        pooled = s / cnt - h[:, lo:lo + gd]
        ys.append(_mm(_act(pooled, hi), (pw[0].at[g], None if pw[1] is None else pw[1].at[g])))
    y = jnp.concatenate(ys, axis=1) * ps_ref[...]
    tail = hp_s[hal + tt - POOL_BUF:hal + tt, :]
    hp_s[1:hal, :] = tail
    pout_ref[0] = tail

    x1 = x + m[2:3, :] * _rms(y, gmp_ref[...])
    x1_ref[0] = x1
    h2 = _rms(x1, gfp_ref[...]) * (1.0 + m[4:5, :]) + m[3:4, :]
    _rows_to_tiles(h2_ref.at[0], h2)

    logits = _mm(_split(h2), (rwh_ref, rwl_ref)) + rb_ref[...]
    lane = lax.broadcasted_iota(jnp.int32, (tt, LANES), 1)
    neg = jnp.float32(-jnp.inf)
    l1 = jnp.where(lane < n_experts, logits, neg)
    m1 = jnp.max(l1, axis=1, keepdims=True)
    i1 = jnp.min(jnp.where(l1 == m1, lane, LANES), axis=1, keepdims=True)
    l2 = jnp.where(lane == i1, neg, l1)
    m2 = jnp.max(l2, axis=1, keepdims=True)
    i2 = jnp.min(jnp.where(l2 == m2, lane, LANES), axis=1, keepdims=True)
    e21 = jnp.exp(m2 - m1)
    g1 = 1.0 / (1.0 + e21)
    g2 = e21 * g1
    sel = ((lane == i1) | (lane == i2)).astype(F32)
    rank = _dot(ltri_ref[...], sel.astype(BF16)) + cnt_s[...]
    r1 = jnp.sum(jnp.where(lane == i1, rank, 0.0), axis=1, keepdims=True)
    r2 = jnp.sum(jnp.where(lane == i2, rank, 0.0), axis=1, keepdims=True)
    cnt_new = cnt_s[...] + jnp.sum(sel, axis=0, keepdims=True)
    cnt_s[...] = cnt_new
    cnt_ref[...] = jnp.broadcast_to(cnt_new, cnt_ref.shape)
    meta = jnp.where(lane == 0, i1.astype(F32), 0.0)
    meta = jnp.where(lane == 1, i2.astype(F32), meta)
    meta = jnp.where(lane == 2, g1, meta)
    meta = jnp.where(lane == 3, g2, meta)
    meta = jnp.where(lane == 4, r1, meta)
    meta = jnp.where(lane == 5, r2, meta)
    meta_ref[0] = meta


def _lower_ones(n):
    i = lax.broadcasted_iota(jnp.int32, (n, n), 0)
    j = lax.broadcasted_iota(jnp.int32, (n, n), 1)
    return (j < i).astype(BF16)


def _pool_router(x, mod, g_pre, pbuf, pool_w, pool_scale, g_mix_post, g_ffn_pre, rw_hi, rw_lo, rb, tt, pos0,
                 n_experts, hi):
    b, t, d = x.shape
    row = lambda i, j: (i, j, 0)
    per_b = lambda i, j: (i, 0, 0)
    return pl.pallas_call(
        functools.partial(_pool_router_kernel, pos0=pos0, n_experts=n_experts, hi=hi),
        grid=(b, t // tt),
        in_specs=[pl.BlockSpec((1, tt, d), row), pl.BlockSpec((1, 6, d), per_b), _const_spec((1, d)),
                  pl.BlockSpec((1, POOL_BUF, d), per_b), _const_spec((1, d)),
                  _const_spec((1, d)), _const_spec((1, d)),
                  _const_spec(rw_hi.shape), _const_spec(rw_lo.shape), _const_spec((1, LANES)),
                  _const_spec((tt, tt))] + _w_specs(pool_w, hi, False),
        out_specs=[pl.BlockSpec((1, tt, d), row),
                   pl.BlockSpec((1, tt, SUBLANES, LANES), lambda i, j: (i, j, 0, 0)),
                   pl.BlockSpec((1, tt, LANES), row), pl.BlockSpec((1, POOL_BUF, d), per_b),
                   _const_spec((8, LANES))],
        out_shape=[jax.ShapeDtypeStruct((b, t, d), F32), jax.ShapeDtypeStruct((b, t, SUBLANES, LANES), F32),
                   jax.ShapeDtypeStruct((b, t, LANES), F32), jax.ShapeDtypeStruct((b, POOL_BUF, d), F32),
                   jax.ShapeDtypeStruct((8, LANES), F32)],
        scratch_shapes=[pltpu.VMEM((tt + POOL_BUF + 1, d), F32), pltpu.VMEM((1, LANES), F32)],
        compiler_params=_cparams(2, VMEM_MID),
        name="pool_router",
    )(x, mod, g_pre, pbuf, pool_scale, g_mix_post, g_ffn_pre, rw_hi, rw_lo, rb, _lower_ones(tt),
      *_w_args(pool_w, hi))


def _start_row_gather(src_ref, idx_ref, idx_base, idx_stride, dst_ref, sem, n_rows):
    def issue(r, c):
        pltpu.make_async_copy(src_ref.at[idx_ref[idx_base + r * idx_stride]], dst_ref.at[r], sem).start()
        return c

    lax.fori_loop(0, n_rows, issue, 0, unroll=8)


def _wait_row_gather(src_ref, dst_ref, sem, n_rows):
    def drain(r, c):
        pltpu.make_async_copy(src_ref.at[0], dst_ref.at[r], sem).wait()
        return c

    lax.fori_loop(0, n_rows, drain, 0, unroll=8)


def _moe_kernel(te_ref, na_ref, tok_ref, h2_ref, wg_ref, wu_ref, wd_ref, ys_ref, xbuf, sem, *, tm):
    i = pl.program_id(0)
    n_active = na_ref[0]
    slot = i % 2

    def start(tile, s):
        _start_row_gather(h2_ref, tok_ref, tile * tm, 1, xbuf.at[s], sem.at[s], tm)

    @pl.when(i == 0)
    def _():
        start(0, 0)

    @pl.when(i + 1 < n_active)
    def _():
        start(i + 1, 1 - slot)

    active = i < n_active

    @pl.when(active)
    def _():
        _wait_row_gather(h2_ref, xbuf.at[slot], sem.at[slot], tm)
        x = _rows_from_tiles(xbuf.at[slot])
        y = _swiglu_rows(_act(x, False), (wg_ref, None), (wu_ref, None), (wd_ref, None), False)
        _rows_to_tiles(ys_ref, y)

    @pl.when(jnp.logical_not(active))
    def _():
        ys_ref[...] = jnp.zeros_like(ys_ref)


def _moe_grouped(h2_tiles, slot_tok, tile_expert, n_active, wg_b, wu_b, wd_b, tm):
    p = slot_tok.shape[0]
    d, ff = wg_b.shape[1], wg_b.shape[2]
    expert = lambda i, te, na, tok: (te[i], 0, 0)
    return pl.pallas_call(
        functools.partial(_moe_kernel, tm=tm),
        grid_spec=pltpu.PrefetchScalarGridSpec(
            num_scalar_prefetch=3,
            grid=(p // tm,),
            in_specs=[pl.BlockSpec(memory_space=pl.ANY),
                      pl.BlockSpec((None, d, ff), expert), pl.BlockSpec((None, d, ff), expert),
                      pl.BlockSpec((None, ff, d), expert)],
            out_specs=pl.BlockSpec((tm, SUBLANES, LANES), lambda i, te, na, tok: (i, 0, 0)),
            scratch_shapes=[pltpu.VMEM((2, tm, SUBLANES, LANES), F32), pltpu.SemaphoreType.DMA((2,))]),
        out_shape=jax.ShapeDtypeStruct((p, SUBLANES, LANES), F32),
        compiler_params=_cparams(1, VMEM_BIG),
        name="moe_grouped_swiglu",
    )(tile_expert, n_active, slot_tok, h2_tiles, wg_b, wu_b, wd_b)


def _combine_kernel(pos_ref, x1_ref, meta_ref, mod_ref, gfo_ref, ys_ref, out_ref, ybuf, sem):
    nb, tt, d = x1_ref.shape
    rows = nb * tt
    step = pl.program_id(0) * pl.num_programs(1) + pl.program_id(1)
    n_steps = pl.num_programs(0) * pl.num_programs(1)
    slot = step % 2

    def start(st, s):
        for k in range(TOP_K):
            _start_row_gather(ys_ref, pos_ref, st * rows * TOP_K + k, TOP_K, ybuf.at[s, k], sem.at[s], rows)

    @pl.when(step == 0)
    def _():
        start(0, 0)

    @pl.when(step + 1 < n_steps)
    def _():
        start(step + 1, 1 - slot)

    for k in range(TOP_K):
        _wait_row_gather(ys_ref, ybuf.at[slot, k], sem.at[slot], rows)
    m = mod_ref[...]
    meta = meta_ref[...]
    ya = _rows_from_tiles(ybuf.at[slot, 0]).reshape(nb, tt, d)
    yb = _rows_from_tiles(ybuf.at[slot, 1]).reshape(nb, tt, d)
    y = meta[:, :, 2:3] * ya + meta[:, :, 3:4] * yb
    out_ref[...] = x1_ref[...] + m[:, 5:6, :] * _rms(y, gfo_ref[...])


def _combine(x1, ys, pos, meta, mod, g_ffn_post, nb, tt):
    b, t, d = x1.shape
    row = lambda i, j, pos_ref: (i, j, 0)
    return pl.pallas_call(
        _combine_kernel,
        grid_spec=pltpu.PrefetchScalarGridSpec(
            num_scalar_prefetch=1,
            grid=(b // nb, t // tt),
            in_specs=[pl.BlockSpec((nb, tt, d), row), pl.BlockSpec((nb, tt, LANES), row),
                      pl.BlockSpec((nb, 6, d), lambda i, j, pos_ref: (i, 0, 0)),
                      pl.BlockSpec((1, d), lambda i, j, pos_ref: (0, 0)),
                      pl.BlockSpec(memory_space=pl.ANY)],
            out_specs=pl.BlockSpec((nb, tt, d), row),
            scratch_shapes=[pltpu.VMEM((2, TOP_K, nb * tt, SUBLANES, LANES), F32),
                            pltpu.SemaphoreType.DMA((2,))]),
        out_shape=jax.ShapeDtypeStruct((b, t, d), F32),
        compiler_params=_cparams(2, VMEM_MID),
        name="moe_combine",
    )(pos, x1, meta, mod, g_ffn_post, ys)


def _block_diag(w):
    heads, hd, _ = w.shape
    eye = jnp.eye(heads, dtype=w.dtype)
    return jnp.einsum("hij,hg->higj", w, eye).reshape(heads * hd, heads * hd)


def _hi_lo(w):
    hi = w.astype(BF16)
    return hi, (w - hi.astype(F32)).astype(BF16)


def _moe_layer(x1, h2, meta, counts, mod, g_ffn_post, wg_b, wu_b, wd_b, tm, nb, tt):
    b, t, d = x1.shape
    n = b * t
    n_experts = wg_b.shape[0]
    n_tiles = (TOP_K * n) // tm + n_experts
    meta2 = meta.reshape(n, LANES)
    e_idx = meta2[:, 0:2].astype(jnp.int32)
    rank = meta2[:, 4:6].astype(jnp.int32)
    cnt = counts[0, :n_experts].astype(jnp.int32)
    tiles_e = (cnt + tm - 1) // tm
    tiles_end = jnp.cumsum(tiles_e)
    group_off = (tiles_end - tiles_e) * tm
    n_active = tiles_end[-1:]
    tile_id = jnp.minimum(jnp.arange(n_tiles, dtype=jnp.int32), n_active[0] - 1)
    tile_expert = jnp.minimum(jnp.sum((tile_id[:, None] >= tiles_end[None, :]).astype(jnp.int32), axis=1),
                              n_experts - 1)
    pos = group_off[e_idx] + rank
    tok = jnp.broadcast_to(jnp.arange(n, dtype=jnp.int32)[:, None], (n, TOP_K))
    slot_tok = jnp.zeros((n_tiles * tm,), jnp.int32).at[pos.reshape(-1)].set(tok.reshape(-1))

    ys = _moe_grouped(h2.reshape(n, SUBLANES, LANES), slot_tok, tile_expert, n_active.astype(jnp.int32),
                      wg_b, wu_b, wd_b, tm)
    return _combine(x1, ys, pos.reshape(-1), meta, mod, g_ffn_post, nb, tt)


def _trunk(x, mod_all, pos0, cache_k, cache_v, st_h, st_conv, st_pool, p, cfg):
    b, t, d = x.shape
    depth = mod_all.shape[0]
    nb, tt = cfg["nb"], cfg["tt"]
    lw = p["conv_w"].shape[2]
    sw = (p["w_in"][0][0].shape[1] - 2 * lw) // 3
    heads = sw // SB_HD
    n_experts = p["moe_wg_b"].shape[1]
    row1 = lambda a: a.reshape(1, -1)
    new_k, new_v, new_h, new_conv, new_pool = [], [], [], [], []
    for l in range(depth):
        e = l // 2
        hi = l in cfg["hi_layers"]
        mod = mod_all[l].reshape(b, 6, d)
        if l % 2 == 0:
            proj = _in_proj(x, mod, row1(p["g_mix_pre"][l]), p["w_in"][e], nb, tt, lw, sw, hi)
            gate, lru_in, q, k32, v32 = proj[:5]
            if cache_k is None:
                cbuf = jnp.zeros((b, CONV_W - 1, lw), F32)
                h0 = jnp.zeros((b, lw), F32)
            else:
                cbuf, h0 = st_conv[e], st_h[e]
            yl, h_last, cb = _lru(lru_in, gate, cbuf, h0, p["conv_w"][e], p["conv_b"][e], p["wa"][e],
                                  p["lru_ba"][e], p["wx"][e], p["lru_bx"][e], p["lru_lambda"][e], cfg["lru_tt"], hi)
            kk, vv = (k32, v32) if hi else proj[5:7]
            if cache_k is None:
                oa = _attn_prompt(q, kk, vv, cfg["attn_tile"], hi)
            else:
                past = cache_k.shape[2]
                oa = _attn_sample(q, kk, vv, cache_k[e].reshape(b, past, sw),
                                  cache_v[e].reshape(b, past, sw), cfg["attn_tile"], hi)
            x = _out_ffn(x, yl, oa, mod, row1(p["g_mix_post"][l]), row1(p["g_ffn_pre"][l]),
                         row1(p["g_ffn_post"][l]), p["w_out"][e], p["ffn_wg"][e], p["ffn_wu"][e],
                         p["ffn_wd"][e], nb, cfg["hi_tt"] if hi else tt, hi)
            new_k.append(k32.reshape(b, t, heads, SB_HD))
            new_v.append(v32.reshape(b, t, heads, SB_HD))
            new_h.append(h_last.reshape(b, lw))
            new_conv.append(cb)
        else:
            pbuf = jnp.zeros((b, POOL_BUF, d), F32) if st_pool is None else st_pool[e]
            x1, h2, meta, pb, counts = _pool_router(
                x, mod, row1(p["g_mix_pre"][l]), pbuf, p["pool_w"][e], row1(p["pool_scale"][e]),
                row1(p["g_mix_post"][l]), row1(p["g_ffn_pre"][l]), p["rw_hi"][e], p["rw_lo"][e], p["rb"][e],
                cfg["pool_tt"], pos0, n_experts, hi)
            x = _moe_layer(x1, h2, meta, counts, mod, row1(p["g_ffn_post"][l]), p["moe_wg_b"][e], p["moe_wu_b"][e],
                           p["moe_wd_b"][e], cfg["moe_tm"], nb, tt)
            new_pool.append(pb)
    return (x, jnp.stack(new_k), jnp.stack(new_v), jnp.stack(new_h), jnp.stack(new_conv), jnp.stack(new_pool))


def kernel(x_prompt, x_sample, cache_sb_k, cache_sb_v, state_lru_h, state_lru_conv, state_pool, c_prompt, c_sample, ada_w, ada_b, g_mix_pre, g_mix_post, g_ffn_pre, g_ffn_post, w_in, w_out, conv_w, conv_b, lru_wa, lru_ba, lru_wx, lru_bx, lru_lambda, pool_w, pool_scale, ffn_w_gate, ffn_w_up, ffn_w_down, router_w, router_b, moe_w_gate, moe_w_up, moe_w_down):
    bp = x_prompt.shape[0]
    n_experts = router_w.shape[2]
    rw_hi, rw_lo = _hi_lo(jnp.pad(router_w, ((0, 0), (0, 0), (0, LANES - n_experts))))

    def per_layer(w):
        return [_hi_lo(w[e]) if e == 0 else (w[e].astype(BF16), None) for e in range(w.shape[0])]

    p = dict(
        g_mix_pre=g_mix_pre, g_mix_post=g_mix_post, g_ffn_pre=g_ffn_pre, g_ffn_post=g_ffn_post,
        w_in=per_layer(w_in), w_out=per_layer(w_out),
        conv_w=conv_w, conv_b=conv_b, lru_ba=lru_ba, lru_bx=lru_bx, lru_lambda=lru_lambda,
        wa=per_layer(jax.vmap(_block_diag)(lru_wa)), wx=per_layer(jax.vmap(_block_diag)(lru_wx)),
        pool_w=per_layer(pool_w), pool_scale=pool_scale,
        ffn_wg=per_layer(ffn_w_gate), ffn_wu=per_layer(ffn_w_up), ffn_wd=per_layer(ffn_w_down),
        rw_hi=rw_hi, rw_lo=rw_lo,
        rb=jnp.pad(router_b, ((0, 0), (0, LANES - n_experts))).reshape(-1, 1, LANES),
        moe_wg_b=moe_w_gate.astype(BF16), moe_wu_b=moe_w_up.astype(BF16), moe_wd_b=moe_w_down.astype(BF16),
    )
    mod_all = _ada_mod(jnp.concatenate([c_prompt, c_sample], axis=0), ada_w, ada_b)
    cfg_prompt = dict(nb=1, tt=512, hi_tt=256, lru_tt=512, pool_tt=512, attn_tile=256, moe_tm=512, hi_layers=(0, 1))
    ts = x_sample.shape[1]
    cfg_sample = dict(nb=x_sample.shape[0] // 2, tt=ts, hi_tt=ts, lru_tt=ts, pool_tt=ts, attn_tile=256, moe_tm=128,
                      hi_layers=(0, 1))
    yp, kp, vp, hp, cp, pp = _trunk(x_prompt, mod_all[:, :bp], 0, None, None, None, None, None, p, cfg_prompt)
    ys, ks, vs, hs, cs, ps = _trunk(x_sample, mod_all[:, bp:], cache_sb_k.shape[2], cache_sb_k, cache_sb_v,
                                    state_lru_h, state_lru_conv, state_pool, p, cfg_sample)
    return (yp, ys, kp, vp, hp, cp, pp, ks, vs, hs, cs, ps)
```

```python
import functools

import jax
import jax.numpy as jnp
from jax import lax
from jax.experimental import pallas as pl
from jax.experimental.pallas import tpu as pltpu

F32 = jnp.float32
BF16 = jnp.bfloat16

EPS = 1e-6
LRU_C = 8.0
CONV_W = 4
SB_HD = 64
POOL_WINDOWS = (2, 4, 8, 16)
POOL_BUF = max(POOL_WINDOWS) - 1
TOP_K = 2
LANES = 128
SUBLANES = 8
FF_CHUNK = 256
EXP_UNDERFLOW = -104.0

VMEM_BIG = 56 * 1024 * 1024
VMEM_MID = 40 * 1024 * 1024


def _cparams(n_axes, vmem=None):
    return pltpu.CompilerParams(dimension_semantics=("arbitrary",) * n_axes, vmem_limit_bytes=vmem)


def _rms(x, g):
    return x * lax.rsqrt(jnp.mean(x * x, axis=-1, keepdims=True) + EPS) * g


def _silu(x):
    return x * jax.nn.sigmoid(x)


def _gelu_tanh(x):
    return 0.5 * x * (1.0 + jnp.tanh(0.7978845608028654 * (x + 0.044715 * (x * x * x))))


def _dot(a, b):
    return jnp.dot(a, b, preferred_element_type=F32)


def _dot_nt(a, b):
    return lax.dot_general(a, b, (((1,), (1,)), ((), ())), preferred_element_type=F32)


def _const_spec(shape, single=False):
    nd = len(shape)
    if single:
        return pl.BlockSpec(shape, lambda *_: (0,) * nd, pipeline_mode=pl.Buffered(1))
    return pl.BlockSpec(shape, lambda *_: (0,) * nd)


def _split(a):
    hi = a.astype(BF16)
    return hi, (a - hi.astype(F32)).astype(BF16)


def _act(a, hi):
    return _split(a) if hi else (a.astype(BF16), None)


def _mm(act, w, rows=slice(None), cols=slice(None)):
    ah, al = act
    wh = w[0][rows, cols]
    out = _dot(ah, wh)
    if al is not None:
        out = out + (_dot(al, wh) + _dot(ah, w[1][rows, cols]))
    return out


def _take_w(refs, hi):
    if hi:
        return (refs[0], refs[1]), refs[2:]
    return (refs[0], None), refs[1:]


def _w_args(w, hi):
    return list(w) if hi else [w[0]]


def _w_specs(w, hi, single=True):
    return [_const_spec(a.shape, single) for a in _w_args(w, hi)]


def _rows_from_tiles(ref):
    return jnp.concatenate([ref[..., s, :] for s in range(SUBLANES)], axis=-1)


def _rows_to_tiles(ref, val):
    for s in range(SUBLANES):
        ref[..., s, :] = val[..., s * LANES:(s + 1) * LANES]


def _ada_kernel(c_ref, w_ref, b_ref, o_ref):
    ch, cl = _split(_silu(c_ref[...]))
    wh, wl = _split(w_ref[...])
    o_ref[...] = _dot(ch, wh) + (_dot(cl, wh) + _dot(ch, wl)) + b_ref[...]


def _ada_mod(c_all, ada_w, ada_b, tn=1536):
    depth, d, n6 = ada_w.shape
    r = c_all.shape[0]
    return pl.pallas_call(
        _ada_kernel,
        grid=(depth, n6 // tn),
        in_specs=[pl.BlockSpec((r, d), lambda l, j: (0, 0)),
                  pl.BlockSpec((None, d, tn), lambda l, j: (l, 0, j)),
                  pl.BlockSpec((None, 1, tn), lambda l, j: (l, 0, j))],
        out_specs=pl.BlockSpec((None, r, tn), lambda l, j: (l, 0, j)),
        out_shape=jax.ShapeDtypeStruct((depth, r, n6), F32),
        compiler_params=_cparams(2, VMEM_MID),
        name="ada_mod",
    )(c_all, ada_w, ada_b.reshape(depth, 1, n6))


def _in_proj_kernel(x_ref, mod_ref, g_ref, *refs, lw, sw, hi):
    w, outs = _take_w(refs, hi)
    nb, tt, d = x_ref.shape
    m = mod_ref[...]
    h = _rms(x_ref[...], g_ref[...]) * (1.0 + m[:, 1:2, :]) + m[:, 0:1, :]
    act = _act(h.reshape(nb * tt, d), hi)

    def proj(c0, c1):
        return _mm(act, w, cols=slice(c0, c1)).reshape(nb, tt, c1 - c0)

    outs[0][...] = proj(0, lw)
    outs[1][...] = proj(lw, 2 * lw)
    outs[2][...] = (proj(2 * lw, 2 * lw + sw) * (SB_HD ** -0.5)).astype(outs[2].dtype)
    k = proj(2 * lw + sw, 2 * lw + 2 * sw)
    v = proj(2 * lw + 2 * sw, 2 * lw + 3 * sw)
    outs[3][...] = k
    outs[4][...] = v
    if not hi:
        outs[5][...] = k.astype(BF16)
        outs[6][...] = v.astype(BF16)


def _in_proj(x, mod, g, w, nb, tt, lw, sw, hi):
    b, t, d = x.shape
    row = lambda i, j: (i, j, 0)
    spec = lambda n: pl.BlockSpec((nb, tt, n), row)
    sds = lambda n, dt: jax.ShapeDtypeStruct((b, t, n), dt)
    out_shape = [sds(lw, F32), sds(lw, F32), sds(sw, F32 if hi else BF16), sds(sw, F32), sds(sw, F32)]
    out_specs = [spec(lw), spec(lw), spec(sw), spec(sw), spec(sw)]
    if not hi:
        out_shape += [sds(sw, BF16), sds(sw, BF16)]
        out_specs += [spec(sw), spec(sw)]
    return pl.pallas_call(
        functools.partial(_in_proj_kernel, lw=lw, sw=sw, hi=hi),
        grid=(b // nb, t // tt),
        in_specs=[spec(d), pl.BlockSpec((nb, 6, d), lambda i, j: (i, 0, 0)), _const_spec((1, d))]
        + _w_specs(w, hi),
        out_specs=out_specs,
        out_shape=out_shape,
        compiler_params=_cparams(2, VMEM_MID),
        name="in_proj",
    )(x, mod, g, *_w_args(w, hi))


def _lru_kernel(xin_ref, gate_ref, cbuf_ref, h0_ref, cw_ref, cb_ref, ba_ref, bx_ref, lam_ref, *refs, hi):
    wa, refs = _take_w(refs, hi)
    wx, refs = _take_w(refs, hi)
    y_ref, hlast_ref, cout_ref, xp_s, a_s, u_s, hs_s, hcar_s = refs
    tt = xin_ref.shape[1]
    pad = 8

    @pl.when(pl.program_id(1) == 0)
    def _():
        xp_s[pad - (CONV_W - 1):pad, :] = cbuf_ref[0]
        hcar_s[...] = h0_ref[0]

    x = xin_ref[0]
    xp_s[pad:pad + tt, :] = x
    w = cw_ref[...]
    xc = cb_ref[...] + xp_s[pad - 3:pad - 3 + tt, :] * w[0:1]
    xc = xc + xp_s[pad - 2:pad - 2 + tt, :] * w[1:2]
    xc = xc + xp_s[pad - 1:pad - 1 + tt, :] * w[2:3]
    xc = xc + x * w[3:4]
    tail = xp_s[pad + tt - (CONV_W - 1):pad + tt, :]
    xp_s[pad - (CONV_W - 1):pad, :] = tail
    cout_ref[0] = tail

    act = _act(xc, hi)
    r = jax.nn.sigmoid(_mm(act, wa) + ba_ref[...])
    i = jax.nn.sigmoid(_mm(act, wx) + bx_ref[...])
    nl = -lam_ref[...]
    softplus = jnp.maximum(nl, 0.0) + jnp.log1p(jnp.exp(-jnp.abs(nl)))
    log_a = (-LRU_C) * r * softplus
    a = jnp.exp(log_a)
    a_s[...] = a
    u_s[...] = jnp.sqrt(jnp.tanh(-log_a) * (a * a + 1.0)) * (i * xc)

    def step(s, h):
        h = a_s[pl.ds(s, 1), :] * h + u_s[pl.ds(s, 1), :]
        hs_s[pl.ds(s, 1), :] = h
        return h

    h = lax.fori_loop(0, tt, step, hcar_s[...], unroll=8)
    hcar_s[...] = h
    hlast_ref[0] = h
    y_ref[0] = (hs_s[...] * _gelu_tanh(gate_ref[0])).astype(y_ref.dtype)


def _lru(lru_in, gate, cbuf, h0, conv_w, conv_b, wa, ba, wx, bx, lam, tt, hi):
    b, t, c = lru_in.shape
    row = lambda i, j: (i, j, 0)
    per_b = lambda i, j: (i, 0, 0)
    vec = lambda a: a.reshape(1, c)
    return pl.pallas_call(
        functools.partial(_lru_kernel, hi=hi),
        grid=(b, t // tt),
        in_specs=[pl.BlockSpec((1, tt, c), row), pl.BlockSpec((1, tt, c), row),
                  pl.BlockSpec((1, CONV_W - 1, c), per_b), pl.BlockSpec((1, 1, c), per_b),
                  _const_spec((CONV_W, c)), _const_spec((1, c)), _const_spec((1, c)), _const_spec((1, c)),
                  _const_spec((1, c))] + _w_specs(wa, hi, False) + _w_specs(wx, hi, False),
        out_specs=[pl.BlockSpec((1, tt, c), row), pl.BlockSpec((1, 1, c), per_b),
                   pl.BlockSpec((1, CONV_W - 1, c), per_b)],
        out_shape=[jax.ShapeDtypeStruct((b, t, c), F32 if hi else BF16), jax.ShapeDtypeStruct((b, 1, c), F32),
                   jax.ShapeDtypeStruct((b, CONV_W - 1, c), F32)],
        scratch_shapes=[pltpu.VMEM((tt + 8, c), F32), pltpu.VMEM((tt, c), F32), pltpu.VMEM((tt, c), F32),
                        pltpu.VMEM((tt, c), F32), pltpu.VMEM((1, c), F32)],
        compiler_params=_cparams(2, VMEM_MID),
        name="rg_lru",
    )(lru_in, gate, cbuf, h0.reshape(b, 1, c), conv_w, vec(conv_b), vec(ba), vec(bx), vec(lam),
      *_w_args(wa, hi), *_w_args(wx, hi))


def _sb_tile(q, k, v, r_in, u, mask, hi):
    z = _dot_nt(q[0], k[0])
    if hi:
        z = z + (_dot_nt(q[1], k[0]) + _dot_nt(q[0], k[1]))
    soft = jnp.log(1.0 + jnp.exp(-jnp.abs(z)))
    log_om = jnp.minimum(-z, 0.0) - soft
    log_beta = log_om + z
    if mask is not None:
        log_om = jnp.where(mask, log_om, 0.0)
    lh, ll = _split(log_om)
    stick = _dot(lh, u) + _dot(ll, u) + r_in
    a = jnp.exp(log_beta + stick)
    if mask is not None:
        a = jnp.where(mask, a, 0.0)
    ah, al = _act(a, hi)
    o = _dot(ah, v[0])
    if hi:
        o = o + (_dot(al, v[0]) + _dot(ah, v[1]))
    return o, r_in + jnp.sum(log_om, axis=1, keepdims=True)


def _attn_prompt_kernel(q_ref, k_ref, v_ref, u_ref, o_ref, *, tile, hi):
    qi = pl.program_id(2)
    n_slab = q_ref.shape[2] // LANES
    per_slab = LANES // SB_HD
    u = u_ref[...]
    lane = lax.broadcasted_iota(jnp.int32, (tile, LANES), 1)
    row = lax.broadcasted_iota(jnp.int32, (tile, tile), 0)
    col = lax.broadcasted_iota(jnp.int32, (tile, tile), 1)
    diag_mask = col < row

    qs = []
    for s in range(n_slab):
        q = q_ref[0, :, s * LANES:(s + 1) * LANES]
        for h in range(per_slab):
            in_head = (lane >= h * SB_HD) & (lane < (h + 1) * SB_HD)
            qs.append(_act(jnp.where(in_head, q, jnp.zeros_like(q)), hi))

    def tiles(j, rs, mask):
        start = pl.multiple_of(j * tile, tile)
        outs = []
        for s in range(n_slab):
            kt = _act(k_ref[0, pl.ds(start, tile), s * LANES:(s + 1) * LANES], hi)
            vt = _act(v_ref[0, pl.ds(start, tile), s * LANES:(s + 1) * LANES], hi)
            for h in range(per_slab):
                i = s * per_slab + h
                outs.append(_sb_tile(qs[i], kt, vt, rs[i], u, mask, hi))
        return [o for o, _ in outs], [r for _, r in outs]

    def r_max(rs):
        m = jnp.max(rs[0])
        for r in rs[1:]:
            m = jnp.maximum(m, jnp.max(r))
        return m

    accs, rs = tiles(qi, [jnp.zeros((tile, 1), F32)] * len(qs), diag_mask)

    def cond(c):
        jj, _, _, m = c
        return (jj < qi) & (m > EXP_UNDERFLOW)

    def body(c):
        jj, accs, rs, _ = c
        outs, rs = tiles(qi - 1 - jj, rs, None)
        return jj + 1, [a + o for a, o in zip(accs, outs)], rs, r_max(rs)

    _, accs, rs, _ = lax.while_loop(cond, body, (jnp.int32(0), accs, rs, r_max(rs)))
    for s in range(n_slab):
        o_ref[0, :, s * LANES:(s + 1) * LANES] = jnp.where(
            lane < SB_HD, accs[s * per_slab], accs[s * per_slab + 1]).astype(o_ref.dtype)


def _upper_ones(n):
    j = lax.broadcasted_iota(jnp.int32, (n, n), 0)
    s = lax.broadcasted_iota(jnp.int32, (n, n), 1)
    return (j > s).astype(BF16)


def _attn_prompt(q, k, v, tile, hi, width=2 * LANES):
    b, t, sw = q.shape
    return pl.pallas_call(
        functools.partial(_attn_prompt_kernel, tile=tile, hi=hi),
        grid=(b, sw // width, t // tile),
        in_specs=[pl.BlockSpec((1, tile, width), lambda i, hp, qi: (i, qi, hp)),
                  pl.BlockSpec((1, t, width), lambda i, hp, qi: (i, 0, hp)),
                  pl.BlockSpec((1, t, width), lambda i, hp, qi: (i, 0, hp)),
                  _const_spec((tile, tile))],
        out_specs=pl.BlockSpec((1, tile, width), lambda i, hp, qi: (i, qi, hp)),
        out_shape=jax.ShapeDtypeStruct((b, t, sw), F32 if hi else BF16),
        compiler_params=_cparams(3, VMEM_MID),
        name="sb_attn_prompt",
    )(q, k, v, _upper_ones(tile))


def _attn_sample_kernel(q_ref, kn_ref, vn_ref, kc_ref, vc_ref, u_ref, o_ref, acc_s, r_s, *, tile, hi):
    tq, sw = q_ref.shape[1], q_ref.shape[2]
    heads = sw // SB_HD
    past = kc_ref.shape[1]
    rows = heads * tq
    q = q_ref[0]
    lane = lax.broadcasted_iota(jnp.int32, (rows, sw), 1)
    row = lax.broadcasted_iota(jnp.int32, (rows, sw), 0)
    keep = lane // SB_HD == row // tq
    qbd = _act(jnp.where(keep, jnp.concatenate([q] * heads, axis=0), jnp.zeros((), q.dtype)), hi)
    u = u_ref[...]

    zpad = jnp.zeros((LANES - tq, sw), kn_ref.dtype)
    kn = _act(jnp.concatenate([kn_ref[0], zpad], axis=0), hi)
    vn = _act(jnp.concatenate([vn_ref[0], zpad], axis=0), hi)
    r_new = lax.broadcasted_iota(jnp.int32, (rows, LANES), 0)
    c_new = lax.broadcasted_iota(jnp.int32, (rows, LANES), 1)
    new_mask = c_new < (r_new % tq)
    acc, r = _sb_tile(qbd, kn, vn, jnp.zeros((rows, 1), F32), u[:LANES, :LANES], new_mask, hi)
    acc_s[...] = acc
    r_s[...] = r
    for j in range(past // tile - 1, -1, -1):
        @pl.when(jnp.max(r_s[...]) > EXP_UNDERFLOW)
        def _(j=j):
            kt = _act(kc_ref[0, j * tile:(j + 1) * tile, :], hi)
            vt = _act(vc_ref[0, j * tile:(j + 1) * tile, :], hi)
            o, r_next = _sb_tile(qbd, kt, vt, r_s[...], u, None, hi)
            acc_s[...] += o
            r_s[...] = r_next
    acc = jnp.where(keep, acc_s[...], 0.0)
    out = acc[0:tq]
    for h in range(1, heads):
        out = out + acc[h * tq:(h + 1) * tq]
    o_ref[0] = out.astype(o_ref.dtype)


def _attn_sample(qb, kb, vb, cache_k, cache_v, tile, hi):
    b, tq, sw = qb.shape
    past = cache_k.shape[1]
    per_b = lambda i: (i, 0, 0)
    return pl.pallas_call(
        functools.partial(_attn_sample_kernel, tile=tile, hi=hi),
        grid=(b,),
        in_specs=[pl.BlockSpec((1, tq, sw), per_b), pl.BlockSpec((1, tq, sw), per_b),
                  pl.BlockSpec((1, tq, sw), per_b),
                  pl.BlockSpec((1, past, sw), per_b), pl.BlockSpec((1, past, sw), per_b),
                  _const_spec((tile, tile))],
        out_specs=pl.BlockSpec((1, tq, sw), per_b),
        out_shape=jax.ShapeDtypeStruct((b, tq, sw), F32 if hi else BF16),
        scratch_shapes=[pltpu.VMEM((sw // SB_HD * tq, sw), F32), pltpu.VMEM((sw // SB_HD * tq, 1), F32)],
        compiler_params=_cparams(1, VMEM_MID),
        name="sb_attn_sample",
    )(qb, kb, vb, cache_k, cache_v, _upper_ones(tile))


def _swiglu_rows(act, wg, wu, wd, hi, between=None):
    ff = wg[0].shape[-1]
    n_chunks = ff // FF_CHUNK
    acc = None
    for c in range(n_chunks):
        cs = slice(c * FF_CHUNK, (c + 1) * FF_CHUNK)
        g = _mm(act, wg, cols=cs)
        up = _mm(act, wu, cols=cs)
        part = _mm(_act(_silu(g) * up, hi), wd, rows=cs)
        acc = part if acc is None else acc + part
        if between is not None:
            between(c, n_chunks)
    return acc


def _out_ffn_kernel(x_ref, yl_ref, oa_ref, mod_ref, gmp_ref, gfp_ref, gfo_ref, *refs, hi):
    wo, refs = _take_w(refs, hi)
    wg, refs = _take_w(refs, hi)
    wu, refs = _take_w(refs, hi)
    wd, refs = _take_w(refs, hi)
    out_ref, = refs
    nb, tt, d = x_ref.shape
    lw = yl_ref.shape[2]
    m = mod_ref[...]
    yl = _act(yl_ref[...].reshape(nb * tt, lw), hi)
    oa = _act(oa_ref[...].reshape(nb * tt, oa_ref.shape[2]), hi)
    y = _mm(yl, wo, rows=slice(0, lw)) + _mm(oa, wo, rows=slice(lw, None))
    x1 = x_ref[...] + m[:, 2:3, :] * _rms(y.reshape(nb, tt, d), gmp_ref[...])
    h = _rms(x1, gfp_ref[...]) * (1.0 + m[:, 4:5, :]) + m[:, 3:4, :]
    f = _swiglu_rows(_act(h.reshape(nb * tt, d), hi), wg, wu, wd, hi)
    out_ref[...] = x1 + m[:, 5:6, :] * _rms(f.reshape(nb, tt, d), gfo_ref[...])


def _out_ffn(x, yl, oa, mod, g_mix_post, g_ffn_pre, g_ffn_post, wo, wg, wu, wd, nb, tt, hi):
    b, t, d = x.shape
    row = lambda i, j: (i, j, 0)
    weights = [wo, wg, wu, wd]
    return pl.pallas_call(
        functools.partial(_out_ffn_kernel, hi=hi),
        grid=(b // nb, t // tt),
        in_specs=[pl.BlockSpec((nb, tt, d), row), pl.BlockSpec((nb, tt, yl.shape[2]), row),
                  pl.BlockSpec((nb, tt, oa.shape[2]), row),
                  pl.BlockSpec((nb, 6, d), lambda i, j: (i, 0, 0)),
                  _const_spec((1, d)), _const_spec((1, d)), _const_spec((1, d))]
        + [s for w in weights for s in _w_specs(w, hi)],
        out_specs=pl.BlockSpec((nb, tt, d), row),
        out_shape=jax.ShapeDtypeStruct((b, t, d), F32),
        compiler_params=_cparams(2, VMEM_BIG),
        name="out_proj_ffn",
    )(x, yl, oa, mod, g_mix_post, g_ffn_pre, g_ffn_post, *[a for w in weights for a in _w_args(w, hi)])


def _pool_router_kernel(x_ref, mod_ref, gpre_ref, pbuf_ref, ps_ref, gmp_ref, gfp_ref,
                        rwh_ref, rwl_ref, rb_ref, ltri_ref, *refs, pos0, n_experts, hi):
    pw, refs = _take_w(refs, hi)
    x1_ref, h2_ref, meta_ref, pout_ref, cnt_ref, hp_s, cnt_s = refs
    tt, d = x_ref.shape[1], x_ref.shape[2]
    hal = POOL_BUF + 1
    bi, ti = pl.program_id(0), pl.program_id(1)

    @pl.when((bi == 0) & (ti == 0))
    def _():
        cnt_s[...] = jnp.zeros_like(cnt_s)

    @pl.when(ti == 0)
    def _():
        hp_s[0:1, :] = jnp.zeros((1, d), F32)
        hp_s[1:hal, :] = pbuf_ref[0]

    m = mod_ref[0]
    x = x_ref[0]
    h = _rms(x, gpre_ref[...]) * (1.0 + m[1:2, :]) + m[0:1, :]
    hp_s[hal:hal + tt, :] = h

    pos = pos0 + ti * tt + lax.broadcasted_iota(jnp.int32, (tt, 1), 0)
    gd = d // len(POOL_WINDOWS)
    ys = []
    for g, w in enumerate(POOL_WINDOWS):
        lo = g * gd
        s = hp_s[hal:hal + tt, lo:lo + gd]
        for back in range(1, w):
            s = s + hp_s[hal - back:hal - back + tt, lo:lo + gd]
        cnt = jnp.minimum(w, pos + 1).astype(F32)
        pooled = s / cnt - h[:, lo:lo + gd]
        ys.append(_mm(_act(pooled, hi), (pw[0].at[g], None if pw[1] is None else pw[1].at[g])))
    y = jnp.concatenate(ys, axis=1) * ps_ref[...]
    tail = hp_s[hal + tt - POOL_BUF:hal + tt, :]
    hp_s[1:hal, :] = tail
    pout_ref[0] = tail

    x1 = x + m[2:3, :] * _rms(y, gmp_ref[...])
    x1_ref[0] = x1
    h2 = _rms(x1, gfp_ref[...]) * (1.0 + m[4:5, :]) + m[3:4, :]
    _rows_to_tiles(h2_ref.at[0], h2)

    logits = _mm(_split(h2), (rwh_ref, rwl_ref)) + rb_ref[...]
    lane = lax.broadcasted_iota(jnp.int32, (tt, LANES), 1)
    neg = jnp.float32(-jnp.inf)
    l1 = jnp.where(lane < n_experts, logits, neg)
    m1 = jnp.max(l1, axis=1, keepdims=True)
    i1 = jnp.min(jnp.where(l1 == m1, lane, LANES), axis=1, keepdims=True)
    l2 = jnp.where(lane == i1, neg, l1)
    m2 = jnp.max(l2, axis=1, keepdims=True)
    i2 = jnp.min(jnp.where(l2 == m2, lane, LANES), axis=1, keepdims=True)
    e21 = jnp.exp(m2 - m1)
    g1 = 1.0 / (1.0 + e21)
    g2 = e21 * g1
    sel = ((lane == i1) | (lane == i2)).astype(F32)
    rank = _dot(ltri_ref[...], sel.astype(BF16)) + cnt_s[...]
    r1 = jnp.sum(jnp.where(lane == i1, rank, 0.0), axis=1, keepdims=True)
    r2 = jnp.sum(jnp.where(lane == i2, rank, 0.0), axis=1, keepdims=True)
    cnt_new = cnt_s[...] + jnp.sum(sel, axis=0, keepdims=True)
    cnt_s[...] = cnt_new
    cnt_ref[...] = jnp.broadcast_to(cnt_new, cnt_ref.shape)
    meta = jnp.where(lane == 0, i1.astype(F32), 0.0)
    meta = jnp.where(lane == 1, i2.astype(F32), meta)
    meta = jnp.where(lane == 2, g1, meta)
    meta = jnp.where(lane == 3, g2, meta)
    meta = jnp.where(lane == 4, r1, meta)
    meta = jnp.where(lane == 5, r2, meta)
    meta_ref[0] = meta


def _lower_ones(n):
    i = lax.broadcasted_iota(jnp.int32, (n, n), 0)
    j = lax.broadcasted_iota(jnp.int32, (n, n), 1)
    return (j < i).astype(BF16)


def _pool_router(x, mod, g_pre, pbuf, pool_w, pool_scale, g_mix_post, g_ffn_pre, rw_hi, rw_lo, rb, tt, pos0,
                 n_experts, hi):
    b, t, d = x.shape
    row = lambda i, j: (i, j, 0)
    per_b = lambda i, j: (i, 0, 0)
    return pl.pallas_call(
        functools.partial(_pool_router_kernel, pos0=pos0, n_experts=n_experts, hi=hi),
        grid=(b, t // tt),
        in_specs=[pl.BlockSpec((1, tt, d), row), pl.BlockSpec((1, 6, d), per_b), _const_spec((1, d)),
                  pl.BlockSpec((1, POOL_BUF, d), per_b), _const_spec((1, d)),
                  _const_spec((1, d)), _const_spec((1, d)),
                  _const_spec(rw_hi.shape), _const_spec(rw_lo.shape), _const_spec((1, LANES)),
                  _const_spec((tt, tt))] + _w_specs(pool_w, hi, False),
        out_specs=[pl.BlockSpec((1, tt, d), row),
                   pl.BlockSpec((1, tt, SUBLANES, LANES), lambda i, j: (i, j, 0, 0)),
                   pl.BlockSpec((1, tt, LANES), row), pl.BlockSpec((1, POOL_BUF, d), per_b),
                   _const_spec((8, LANES))],
        out_shape=[jax.ShapeDtypeStruct((b, t, d), F32), jax.ShapeDtypeStruct((b, t, SUBLANES, LANES), F32),
                   jax.ShapeDtypeStruct((b, t, LANES), F32), jax.ShapeDtypeStruct((b, POOL_BUF, d), F32),
                   jax.ShapeDtypeStruct((8, LANES), F32)],
        scratch_shapes=[pltpu.VMEM((tt + POOL_BUF + 1, d), F32), pltpu.VMEM((1, LANES), F32)],
        compiler_params=_cparams(2, VMEM_MID),
        name="pool_router",
    )(x, mod, g_pre, pbuf, pool_scale, g_mix_post, g_ffn_pre, rw_hi, rw_lo, rb, _lower_ones(tt),
      *_w_args(pool_w, hi))


def _start_row_gather(src_ref, idx_ref, idx_base, idx_stride, dst_ref, sem, n_rows):
    def issue(r, c):
        pltpu.make_async_copy(src_ref.at[idx_ref[idx_base + r * idx_stride]], dst_ref.at[r], sem).start()
        return c

    lax.fori_loop(0, n_rows, issue, 0, unroll=8)


def _wait_row_gather(src_ref, dst_ref, sem, n_rows):
    def drain(r, c):
        pltpu.make_async_copy(src_ref.at[0], dst_ref.at[r], sem).wait()
        return c

    lax.fori_loop(0, n_rows, drain, 0, unroll=8)


def _moe_kernel(te_ref, na_ref, tok_ref, h2_ref, wg_ref, wu_ref, wd_ref, ys_ref, xbuf, sem, *, tm):
    i = pl.program_id(0)
    n_active = na_ref[0]
    slot = i % 2

    @pl.when(i == 0)
    def _():
        _start_row_gather(h2_ref, tok_ref, 0, 1, xbuf.at[0], sem.at[0], tm)

    def compute(prefetch):
        def issue_share(c, n):
            for r in range(c * tm // n, (c + 1) * tm // n):
                pltpu.make_async_copy(h2_ref.at[tok_ref[(i + 1) * tm + r]], xbuf.at[1 - slot, r],
                                      sem.at[1 - slot]).start()

        _wait_row_gather(h2_ref, xbuf.at[slot], sem.at[slot], tm)
        x = _rows_from_tiles(xbuf.at[slot])
        y = _swiglu_rows(_act(x, False), (wg_ref, None), (wu_ref, None), (wd_ref, None), False,
                         between=issue_share if prefetch else None)
        _rows_to_tiles(ys_ref, y)

    @pl.when(i + 1 < n_active)
    def _():
        compute(True)

    @pl.when(i + 1 == n_active)
    def _():
        compute(False)

    @pl.when(i >= n_active)
    def _():
        ys_ref[...] = jnp.zeros_like(ys_ref)


def _moe_grouped(h2_tiles, slot_tok, tile_expert, n_active, wg_b, wu_b, wd_b, tm):
    p = slot_tok.shape[0]
    d, ff = wg_b.shape[1], wg_b.shape[2]
    expert = lambda i, te, na, tok: (te[i], 0, 0)
    return pl.pallas_call(
        functools.partial(_moe_kernel, tm=tm),
        grid_spec=pltpu.PrefetchScalarGridSpec(
            num_scalar_prefetch=3,
            grid=(p // tm,),
            in_specs=[pl.BlockSpec(memory_space=pl.ANY),
                      pl.BlockSpec((None, d, ff), expert), pl.BlockSpec((None, d, ff), expert),
                      pl.BlockSpec((None, ff, d), expert)],
            out_specs=pl.BlockSpec((tm, SUBLANES, LANES), lambda i, te, na, tok: (i, 0, 0)),
            scratch_shapes=[pltpu.VMEM((2, tm, SUBLANES, LANES), F32), pltpu.SemaphoreType.DMA((2,))]),
        out_shape=jax.ShapeDtypeStruct((p, SUBLANES, LANES), F32),
        compiler_params=_cparams(1, VMEM_BIG),
        name="moe_grouped_swiglu",
    )(tile_expert, n_active, slot_tok, h2_tiles, wg_b, wu_b, wd_b)


def _combine_kernel(pos_ref, x1_ref, meta_ref, mod_ref, gfo_ref, ys_ref, out_ref, ybuf, sem):
    nb, tt, d = x1_ref.shape
    rows = nb * tt
    step = pl.program_id(0) * pl.num_programs(1) + pl.program_id(1)
    n_steps = pl.num_programs(0) * pl.num_programs(1)
    slot = step % 2

    @pl.when(step == 0)
    def _():
        for k in range(TOP_K):
            _start_row_gather(ys_ref, pos_ref, k, TOP_K, ybuf.at[0, k], sem.at[0], rows)

    def compute(prefetch):
        for k in range(TOP_K):
            _wait_row_gather(ys_ref, ybuf.at[slot, k], sem.at[slot], rows)
        if prefetch:
            for r in range(rows):
                for k in range(TOP_K):
                    pltpu.make_async_copy(ys_ref.at[pos_ref[((step + 1) * rows + r) * TOP_K + k]],
                                          ybuf.at[1 - slot, k, r], sem.at[1 - slot]).start()
        m = mod_ref[...]
        meta = meta_ref[...]
        ya = _rows_from_tiles(ybuf.at[slot, 0]).reshape(nb, tt, d)
        yb = _rows_from_tiles(ybuf.at[slot, 1]).reshape(nb, tt, d)
        y = meta[:, :, 2:3] * ya + meta[:, :, 3:4] * yb
        out_ref[...] = x1_ref[...] + m[:, 5:6, :] * _rms(y, gfo_ref[...])

    @pl.when(step + 1 < n_steps)
    def _():
        compute(True)

    @pl.when(step + 1 == n_steps)
    def _():
        compute(False)


def _combine(x1, ys, pos, meta, mod, g_ffn_post, nb, tt):
    b, t, d = x1.shape
    row = lambda i, j, pos_ref: (i, j, 0)
    return pl.pallas_call(
        _combine_kernel,
        grid_spec=pltpu.PrefetchScalarGridSpec(
            num_scalar_prefetch=1,
            grid=(b // nb, t // tt),
            in_specs=[pl.BlockSpec((nb, tt, d), row), pl.BlockSpec((nb, tt, LANES), row),
                      pl.BlockSpec((nb, 6, d), lambda i, j, pos_ref: (i, 0, 0)),
                      pl.BlockSpec((1, d), lambda i, j, pos_ref: (0, 0)),
                      pl.BlockSpec(memory_space=pl.ANY)],
            out_specs=pl.BlockSpec((nb, tt, d), row),
            scratch_shapes=[pltpu.VMEM((2, TOP_K, nb * tt, SUBLANES, LANES), F32),
                            pltpu.SemaphoreType.DMA((2,))]),
        out_shape=jax.ShapeDtypeStruct((b, t, d), F32),
        compiler_params=_cparams(2, VMEM_MID),
        name="moe_combine",
    )(pos, x1, meta, mod, g_ffn_post, ys)


def _block_diag(w):
    heads, hd, _ = w.shape
    eye = jnp.eye(heads, dtype=w.dtype)
    return jnp.einsum("hij,hg->higj", w, eye).reshape(heads * hd, heads * hd)


def _hi_lo(w):
    hi = w.astype(BF16)
    return hi, (w - hi.astype(F32)).astype(BF16)


def _moe_layer(x1, h2, meta, counts, mod, g_ffn_post, wg_b, wu_b, wd_b, tm, nb, tt):
    b, t, d = x1.shape
    n = b * t
    n_experts = wg_b.shape[0]
    n_tiles = (TOP_K * n) // tm + n_experts
    meta2 = meta.reshape(n, LANES)
    e_idx = meta2[:, 0:2].astype(jnp.int32)
    rank = meta2[:, 4:6].astype(jnp.int32)
    cnt = counts[0, :n_experts].astype(jnp.int32)
    tiles_e = (cnt + tm - 1) // tm
    tiles_end = jnp.cumsum(tiles_e)
    group_off = (tiles_end - tiles_e) * tm
    n_active = tiles_end[-1:]
    tile_id = jnp.minimum(jnp.arange(n_tiles, dtype=jnp.int32), n_active[0] - 1)
    tile_expert = jnp.minimum(jnp.sum((tile_id[:, None] >= tiles_end[None, :]).astype(jnp.int32), axis=1),
                              n_experts - 1)
    pos = group_off[e_idx] + rank
    tok = jnp.broadcast_to(jnp.arange(n, dtype=jnp.int32)[:, None], (n, TOP_K))
    slot_tok = jnp.zeros((n_tiles * tm,), jnp.int32).at[pos.reshape(-1)].set(tok.reshape(-1))

    ys = _moe_grouped(h2.reshape(n, SUBLANES, LANES), slot_tok, tile_expert, n_active.astype(jnp.int32),
                      wg_b, wu_b, wd_b, tm)
    return _combine(x1, ys, pos.reshape(-1), meta, mod, g_ffn_post, nb, tt)


def _trunk(x, mod_all, pos0, cache_k, cache_v, st_h, st_conv, st_pool, p, cfg):
    b, t, d = x.shape
    depth = mod_all.shape[0]
    nb, tt = cfg["nb"], cfg["tt"]
    lw = p["conv_w"].shape[2]
    sw = (p["w_in"][0][0].shape[1] - 2 * lw) // 3
    heads = sw // SB_HD
    n_experts = p["moe_wg_b"].shape[1]
    row1 = lambda a: a.reshape(1, -1)
    new_k, new_v, new_h, new_conv, new_pool = [], [], [], [], []
    for l in range(depth):
        e = l // 2
        hi = l in cfg["hi_layers"]
        mod = mod_all[l].reshape(b, 6, d)
        if l % 2 == 0:
            proj = _in_proj(x, mod, row1(p["g_mix_pre"][l]), p["w_in"][e], nb, tt, lw, sw, hi)
            gate, lru_in, q, k32, v32 = proj[:5]
            if cache_k is None:
                cbuf = jnp.zeros((b, CONV_W - 1, lw), F32)
                h0 = jnp.zeros((b, lw), F32)
            else:
                cbuf, h0 = st_conv[e], st_h[e]
            yl, h_last, cb = _lru(lru_in, gate, cbuf, h0, p["conv_w"][e], p["conv_b"][e], p["wa"][e],
                                  p["lru_ba"][e], p["wx"][e], p["lru_bx"][e], p["lru_lambda"][e], cfg["lru_tt"], hi)
            kk, vv = (k32, v32) if hi else proj[5:7]
            if cache_k is None:
                oa = _attn_prompt(q, kk, vv, cfg["attn_tile"], hi)
            else:
                past = cache_k.shape[2]
                oa = _attn_sample(q, kk, vv, cache_k[e].reshape(b, past, sw),
                                  cache_v[e].reshape(b, past, sw), cfg["attn_tile"], hi)
            x = _out_ffn(x, yl, oa, mod, row1(p["g_mix_post"][l]), row1(p["g_ffn_pre"][l]),
                         row1(p["g_ffn_post"][l]), p["w_out"][e], p["ffn_wg"][e], p["ffn_wu"][e],
                         p["ffn_wd"][e], nb, cfg["hi_tt"] if hi else tt, hi)
            new_k.append(k32.reshape(b, t, heads, SB_HD))
            new_v.append(v32.reshape(b, t, heads, SB_HD))
            new_h.append(h_last.reshape(b, lw))
            new_conv.append(cb)
        else:
            pbuf = jnp.zeros((b, POOL_BUF, d), F32) if st_pool is None else st_pool[e]
            x1, h2, meta, pb, counts = _pool_router(
                x, mod, row1(p["g_mix_pre"][l]), pbuf, p["pool_w"][e], row1(p["pool_scale"][e]),
                row1(p["g_mix_post"][l]), row1(p["g_ffn_pre"][l]), p["rw_hi"][e], p["rw_lo"][e], p["rb"][e],
                cfg["pool_tt"], pos0, n_experts, hi)
            x = _moe_layer(x1, h2, meta, counts, mod, row1(p["g_ffn_post"][l]), p["moe_wg_b"][e], p["moe_wu_b"][e],
                           p["moe_wd_b"][e], cfg["moe_tm"], nb, tt)
            new_pool.append(pb)
    return (x, jnp.stack(new_k), jnp.stack(new_v), jnp.stack(new_h), jnp.stack(new_conv), jnp.stack(new_pool))


def kernel(x_prompt, x_sample, cache_sb_k, cache_sb_v, state_lru_h, state_lru_conv, state_pool, c_prompt, c_sample, ada_w, ada_b, g_mix_pre, g_mix_post, g_ffn_pre, g_ffn_post, w_in, w_out, conv_w, conv_b, lru_wa, lru_ba, lru_wx, lru_bx, lru_lambda, pool_w, pool_scale, ffn_w_gate, ffn_w_up, ffn_w_down, router_w, router_b, moe_w_gate, moe_w_up, moe_w_down):
    bp = x_prompt.shape[0]
    n_experts = router_w.shape[2]
    rw_hi, rw_lo = _hi_lo(jnp.pad(router_w, ((0, 0), (0, 0), (0, LANES - n_experts))))

    def per_layer(w):
        return [_hi_lo(w[e]) if e == 0 else (w[e].astype(BF16), None) for e in range(w.shape[0])]

    p = dict(
        g_mix_pre=g_mix_pre, g_mix_post=g_mix_post, g_ffn_pre=g_ffn_pre, g_ffn_post=g_ffn_post,
        w_in=per_layer(w_in), w_out=per_layer(w_out),
        conv_w=conv_w, conv_b=conv_b, lru_ba=lru_ba, lru_bx=lru_bx, lru_lambda=lru_lambda,
        wa=per_layer(jax.vmap(_block_diag)(lru_wa)), wx=per_layer(jax.vmap(_block_diag)(lru_wx)),
        pool_w=per_layer(pool_w), pool_scale=pool_scale,
        ffn_wg=per_layer(ffn_w_gate), ffn_wu=per_layer(ffn_w_up), ffn_wd=per_layer(ffn_w_down),
        rw_hi=rw_hi, rw_lo=rw_lo,
        rb=jnp.pad(router_b, ((0, 0), (0, LANES - n_experts))).reshape(-1, 1, LANES),
        moe_wg_b=moe_w_gate.astype(BF16), moe_wu_b=moe_w_up.astype(BF16), moe_wd_b=moe_w_down.astype(BF16),
    )
    mod_all = _ada_mod(jnp.concatenate([c_prompt, c_sample], axis=0), ada_w, ada_b)
    cfg_prompt = dict(nb=1, tt=512, hi_tt=256, lru_tt=512, pool_tt=512, attn_tile=256, moe_tm=512, hi_layers=(0, 1))
    ts = x_sample.shape[1]
    cfg_sample = dict(nb=x_sample.shape[0] // 2, tt=ts, hi_tt=ts, lru_tt=ts, pool_tt=ts, attn_tile=256, moe_tm=128,
                      hi_layers=(0, 1))
    yp, kp, vp, hp, cp, pp = _trunk(x_prompt, mod_all[:, :bp], 0, None, None, None, None, None, p, cfg_prompt)
    ys, ks, vs, hs, cs, ps = _trunk(x_sample, mod_all[:, bp:], cache_sb_k.shape[2], cache_sb_k, cache_sb_v,
                                    state_lru_h, state_lru_conv, state_pool, p, cfg_sample)
    return (yp, ys, kp, vp, hp, cp, pp, ks, vs, hs, cs, ps)
```

```python
import functools

import jax
import jax.numpy as jnp
from jax import lax
from jax.experimental import pallas as pl
from jax.experimental.pallas import tpu as pltpu

F32 = jnp.float32
BF16 = jnp.bfloat16

EPS = 1e-6
LRU_C = 8.0
CONV_W = 4
SB_HD = 64
POOL_WINDOWS = (2, 4, 8, 16)
POOL_BUF = max(POOL_WINDOWS) - 1
TOP_K = 2
LANES = 128
SUBLANES = 8
FF_CHUNK = 256
EXP_UNDERFLOW = -104.0

VMEM_BIG = 56 * 1024 * 1024
VMEM_MID = 40 * 1024 * 1024


def _cparams(n_axes, vmem=None):
    return pltpu.CompilerParams(dimension_semantics=("arbitrary",) * n_axes, vmem_limit_bytes=vmem)


def _rms(x, g):
    return x * lax.rsqrt(jnp.mean(x * x, axis=-1, keepdims=True) + EPS) * g


def _silu(x):
    return x * jax.nn.sigmoid(x)


def _gelu_tanh(x):
    return 0.5 * x * (1.0 + jnp.tanh(0.7978845608028654 * (x + 0.044715 * (x * x * x))))


def _dot(a, b):
    return jnp.dot(a, b, preferred_element_type=F32)


def _dot_nt(a, b):
    return lax.dot_general(a, b, (((1,), (1,)), ((), ())), preferred_element_type=F32)


def _const_spec(shape, single=False):
    nd = len(shape)
    if single:
        return pl.BlockSpec(shape, lambda *_: (0,) * nd, pipeline_mode=pl.Buffered(1))
    return pl.BlockSpec(shape, lambda *_: (0,) * nd)


def _split(a):
    hi = a.astype(BF16)
    return hi, (a - hi.astype(F32)).astype(BF16)


def _act(a, hi):
    return _split(a) if hi else (a.astype(BF16), None)


def _mm(act, w, rows=slice(None), cols=slice(None)):
    ah, al = act
    wh = w[0][rows, cols]
    out = _dot(ah, wh)
    if al is not None:
        out = out + (_dot(al, wh) + _dot(ah, w[1][rows, cols]))
    return out


def _take_w(refs, hi):
    if hi:
        return (refs[0], refs[1]), refs[2:]
    return (refs[0], None), refs[1:]


def _w_args(w, hi):
    return list(w) if hi else [w[0]]


def _w_specs(w, hi, single=True):
    return [_const_spec(a.shape, single) for a in _w_args(w, hi)]


def _tile_of(ref, r):
    return ref.at[pl.ds(pl.multiple_of(r * SUBLANES, SUBLANES), SUBLANES), :]


def _rows_from_tiles(ref, rows):
    return jnp.concatenate([ref[pl.ds(s, rows, stride=SUBLANES), :] for s in range(SUBLANES)], axis=-1)


def _rows_to_tiles(ref, val):
    for s in range(SUBLANES):
        ref[pl.ds(s, val.shape[0], stride=SUBLANES), :] = val[:, s * LANES:(s + 1) * LANES]


def _ada_kernel(c_ref, w_ref, b_ref, o_ref):
    ch, cl = _split(_silu(c_ref[...]))
    wh, wl = _split(w_ref[...])
    o_ref[...] = _dot(ch, wh) + (_dot(cl, wh) + _dot(ch, wl)) + b_ref[...]


def _ada_mod(c_all, ada_w, ada_b, tn=1536):
    depth, d, n6 = ada_w.shape
    r = c_all.shape[0]
    return pl.pallas_call(
        _ada_kernel,
        grid=(depth, n6 // tn),
        in_specs=[pl.BlockSpec((r, d), lambda l, j: (0, 0)),
                  pl.BlockSpec((None, d, tn), lambda l, j: (l, 0, j)),
                  pl.BlockSpec((None, 1, tn), lambda l, j: (l, 0, j))],
        out_specs=pl.BlockSpec((None, r, tn), lambda l, j: (l, 0, j)),
        out_shape=jax.ShapeDtypeStruct((depth, r, n6), F32),
        compiler_params=_cparams(2, VMEM_MID),
        name="ada_mod",
    )(c_all, ada_w, ada_b.reshape(depth, 1, n6))


def _in_proj_kernel(x_ref, mod_ref, g_ref, *refs, lw, sw, hi):
    w, outs = _take_w(refs, hi)
    nb, tt, d = x_ref.shape
    m = mod_ref[...]
    h = _rms(x_ref[...], g_ref[...]) * (1.0 + m[:, 1:2, :]) + m[:, 0:1, :]
    act = _act(h.reshape(nb * tt, d), hi)

    def proj(c0, c1):
        return _mm(act, w, cols=slice(c0, c1)).reshape(nb, tt, c1 - c0)

    outs[0][...] = proj(0, lw)
    outs[1][...] = proj(lw, 2 * lw)
    outs[2][...] = (proj(2 * lw, 2 * lw + sw) * (SB_HD ** -0.5)).astype(outs[2].dtype)
    k = proj(2 * lw + sw, 2 * lw + 2 * sw)
    v = proj(2 * lw + 2 * sw, 2 * lw + 3 * sw)
    outs[3][...] = k
    outs[4][...] = v
    if not hi:
        outs[5][...] = k.astype(BF16)
        outs[6][...] = v.astype(BF16)


def _in_proj(x, mod, g, w, nb, tt, lw, sw, hi):
    b, t, d = x.shape
    row = lambda i, j: (i, j, 0)
    spec = lambda n: pl.BlockSpec((nb, tt, n), row)
    sds = lambda n, dt: jax.ShapeDtypeStruct((b, t, n), dt)
    out_shape = [sds(lw, F32), sds(lw, F32), sds(sw, F32 if hi else BF16), sds(sw, F32), sds(sw, F32)]
    out_specs = [spec(lw), spec(lw), spec(sw), spec(sw), spec(sw)]
    if not hi:
        out_shape += [sds(sw, BF16), sds(sw, BF16)]
        out_specs += [spec(sw), spec(sw)]
    return pl.pallas_call(
        functools.partial(_in_proj_kernel, lw=lw, sw=sw, hi=hi),
        grid=(b // nb, t // tt),
        in_specs=[spec(d), pl.BlockSpec((nb, 6, d), lambda i, j: (i, 0, 0)), _const_spec((1, d))]
        + _w_specs(w, hi),
        out_specs=out_specs,
        out_shape=out_shape,
        compiler_params=_cparams(2, VMEM_MID),
        name="in_proj",
    )(x, mod, g, *_w_args(w, hi))


def _lru_kernel(xin_ref, gate_ref, cbuf_ref, h0_ref, cw_ref, cb_ref, ba_ref, bx_ref, lam_ref, *refs, hi):
    wa, refs = _take_w(refs, hi)
    wx, refs = _take_w(refs, hi)
    y_ref, hlast_ref, cout_ref, xp_s, a_s, u_s, hs_s, hcar_s = refs
    tt = xin_ref.shape[1]
    pad = 8

    @pl.when(pl.program_id(1) == 0)
    def _():
        xp_s[pad - (CONV_W - 1):pad, :] = cbuf_ref[0]
        hcar_s[...] = h0_ref[0]

    x = xin_ref[0]
    xp_s[pad:pad + tt, :] = x
    w = cw_ref[...]
    xc = cb_ref[...] + xp_s[pad - 3:pad - 3 + tt, :] * w[0:1]
    xc = xc + xp_s[pad - 2:pad - 2 + tt, :] * w[1:2]
    xc = xc + xp_s[pad - 1:pad - 1 + tt, :] * w[2:3]
    xc = xc + x * w[3:4]
    tail = xp_s[pad + tt - (CONV_W - 1):pad + tt, :]
    xp_s[pad - (CONV_W - 1):pad, :] = tail
    cout_ref[0] = tail

    act = _act(xc, hi)
    r = jax.nn.sigmoid(_mm(act, wa) + ba_ref[...])
    i = jax.nn.sigmoid(_mm(act, wx) + bx_ref[...])
    nl = -lam_ref[...]
    softplus = jnp.maximum(nl, 0.0) + jnp.log1p(jnp.exp(-jnp.abs(nl)))
    log_a = (-LRU_C) * r * softplus
    a = jnp.exp(log_a)
    a_s[...] = a
    u_s[...] = jnp.sqrt(jnp.tanh(-log_a) * (a * a + 1.0)) * (i * xc)

    def step(s, h):
        h = a_s[pl.ds(s, 1), :] * h + u_s[pl.ds(s, 1), :]
        hs_s[pl.ds(s, 1), :] = h
        return h

    h = lax.fori_loop(0, tt, step, hcar_s[...], unroll=8)
    hcar_s[...] = h
    hlast_ref[0] = h
    y_ref[0] = (hs_s[...] * _gelu_tanh(gate_ref[0])).astype(y_ref.dtype)


def _lru(lru_in, gate, cbuf, h0, conv_w, conv_b, wa, ba, wx, bx, lam, tt, hi):
    b, t, c = lru_in.shape
    row = lambda i, j: (i, j, 0)
    per_b = lambda i, j: (i, 0, 0)
    vec = lambda a: a.reshape(1, c)
    return pl.pallas_call(
        functools.partial(_lru_kernel, hi=hi),
        grid=(b, t // tt),
        in_specs=[pl.BlockSpec((1, tt, c), row), pl.BlockSpec((1, tt, c), row),
                  pl.BlockSpec((1, CONV_W - 1, c), per_b), pl.BlockSpec((1, 1, c), per_b),
                  _const_spec((CONV_W, c)), _const_spec((1, c)), _const_spec((1, c)), _const_spec((1, c)),
                  _const_spec((1, c))] + _w_specs(wa, hi, False) + _w_specs(wx, hi, False),
        out_specs=[pl.BlockSpec((1, tt, c), row), pl.BlockSpec((1, 1, c), per_b),
                   pl.BlockSpec((1, CONV_W - 1, c), per_b)],
        out_shape=[jax.ShapeDtypeStruct((b, t, c), F32 if hi else BF16), jax.ShapeDtypeStruct((b, 1, c), F32),
                   jax.ShapeDtypeStruct((b, CONV_W - 1, c), F32)],
        scratch_shapes=[pltpu.VMEM((tt + 8, c), F32), pltpu.VMEM((tt, c), F32), pltpu.VMEM((tt, c), F32),
                        pltpu.VMEM((tt, c), F32), pltpu.VMEM((1, c), F32)],
        compiler_params=_cparams(2, VMEM_MID),
        name="rg_lru",
    )(lru_in, gate, cbuf, h0.reshape(b, 1, c), conv_w, vec(conv_b), vec(ba), vec(bx), vec(lam),
      *_w_args(wa, hi), *_w_args(wx, hi))


def _sb_tile(q, k, v, r_in, u, mask, hi):
    z = _dot_nt(q[0], k[0])
    if hi:
        z = z + (_dot_nt(q[1], k[0]) + _dot_nt(q[0], k[1]))
    soft = jnp.log(1.0 + jnp.exp(-jnp.abs(z)))
    log_om = jnp.minimum(-z, 0.0) - soft
    log_beta = log_om + z
    if mask is not None:
        log_om = jnp.where(mask, log_om, 0.0)
    lh, ll = _split(log_om)
    stick = _dot(lh, u) + _dot(ll, u) + r_in
    a = jnp.exp(log_beta + stick)
    if mask is not None:
        a = jnp.where(mask, a, 0.0)
    ah, al = _act(a, hi)
    o = _dot(ah, v[0])
    if hi:
        o = o + (_dot(al, v[0]) + _dot(ah, v[1]))
    return o, r_in + jnp.sum(log_om, axis=1, keepdims=True)


def _attn_prompt_kernel(q_ref, k_ref, v_ref, u_ref, o_ref, *, tile, hi):
    qi = pl.program_id(2)
    n_slab = q_ref.shape[2] // LANES
    per_slab = LANES // SB_HD
    u = u_ref[...]
    lane = lax.broadcasted_iota(jnp.int32, (tile, LANES), 1)
    row = lax.broadcasted_iota(jnp.int32, (tile, tile), 0)
    col = lax.broadcasted_iota(jnp.int32, (tile, tile), 1)
    diag_mask = col < row

    qs = []
    for s in range(n_slab):
        q = q_ref[0, :, s * LANES:(s + 1) * LANES]
        for h in range(per_slab):
            in_head = (lane >= h * SB_HD) & (lane < (h + 1) * SB_HD)
            qs.append(_act(jnp.where(in_head, q, jnp.zeros_like(q)), hi))

    def tiles(j, rs, mask):
        start = pl.multiple_of(j * tile, tile)
        outs = []
        for s in range(n_slab):
            kt = _act(k_ref[0, pl.ds(start, tile), s * LANES:(s + 1) * LANES], hi)
            vt = _act(v_ref[0, pl.ds(start, tile), s * LANES:(s + 1) * LANES], hi)
            for h in range(per_slab):
                i = s * per_slab + h
                outs.append(_sb_tile(qs[i], kt, vt, rs[i], u, mask, hi))
        return [o for o, _ in outs], [r for _, r in outs]

    def r_max(rs):
        m = jnp.max(rs[0])
        for r in rs[1:]:
            m = jnp.maximum(m, jnp.max(r))
        return m

    accs, rs = tiles(qi, [jnp.zeros((tile, 1), F32)] * len(qs), diag_mask)

    def cond(c):
        jj, _, _, m = c
        return (jj < qi) & (m > EXP_UNDERFLOW)

    def body(c):
        jj, accs, rs, _ = c
        outs, rs = tiles(qi - 1 - jj, rs, None)
        return jj + 1, [a + o for a, o in zip(accs, outs)], rs, r_max(rs)

    _, accs, rs, _ = lax.while_loop(cond, body, (jnp.int32(0), accs, rs, r_max(rs)))
    for s in range(n_slab):
        o_ref[0, :, s * LANES:(s + 1) * LANES] = jnp.where(
            lane < SB_HD, accs[s * per_slab], accs[s * per_slab + 1]).astype(o_ref.dtype)


def _upper_ones(n):
    j = lax.broadcasted_iota(jnp.int32, (n, n), 0)
    s = lax.broadcasted_iota(jnp.int32, (n, n), 1)
    return (j > s).astype(BF16)


def _attn_prompt(q, k, v, tile, hi, width=2 * LANES):
    b, t, sw = q.shape
    return pl.pallas_call(
        functools.partial(_attn_prompt_kernel, tile=tile, hi=hi),
        grid=(b, sw // width, t // tile),
        in_specs=[pl.BlockSpec((1, tile, width), lambda i, hp, qi: (i, qi, hp)),
                  pl.BlockSpec((1, t, width), lambda i, hp, qi: (i, 0, hp)),
                  pl.BlockSpec((1, t, width), lambda i, hp, qi: (i, 0, hp)),
                  _const_spec((tile, tile))],
        out_specs=pl.BlockSpec((1, tile, width), lambda i, hp, qi: (i, qi, hp)),
        out_shape=jax.ShapeDtypeStruct((b, t, sw), F32 if hi else BF16),
        compiler_params=_cparams(3, VMEM_MID),
        name="sb_attn_prompt",
    )(q, k, v, _upper_ones(tile))


def _attn_sample_kernel(q_ref, kn_ref, vn_ref, kc_ref, vc_ref, u_ref, o_ref, acc_s, r_s, *, tile, hi):
    tq, sw = q_ref.shape[1], q_ref.shape[2]
    heads = sw // SB_HD
    past = kc_ref.shape[1]
    rows = heads * tq
    q = q_ref[0]
    lane = lax.broadcasted_iota(jnp.int32, (rows, sw), 1)
    row = lax.broadcasted_iota(jnp.int32, (rows, sw), 0)
    keep = lane // SB_HD == row // tq
    qbd = _act(jnp.where(keep, jnp.concatenate([q] * heads, axis=0), jnp.zeros((), q.dtype)), hi)
    u = u_ref[...]

    zpad = jnp.zeros((LANES - tq, sw), kn_ref.dtype)
    kn = _act(jnp.concatenate([kn_ref[0], zpad], axis=0), hi)
    vn = _act(jnp.concatenate([vn_ref[0], zpad], axis=0), hi)
    r_new = lax.broadcasted_iota(jnp.int32, (rows, LANES), 0)
    c_new = lax.broadcasted_iota(jnp.int32, (rows, LANES), 1)
    new_mask = c_new < (r_new % tq)
    acc, r = _sb_tile(qbd, kn, vn, jnp.zeros((rows, 1), F32), u[:LANES, :LANES], new_mask, hi)
    acc_s[...] = acc
    r_s[...] = r
    for j in range(past // tile - 1, -1, -1):
        @pl.when(jnp.max(r_s[...]) > EXP_UNDERFLOW)
        def _(j=j):
            kt = _act(kc_ref[0, j * tile:(j + 1) * tile, :], hi)
            vt = _act(vc_ref[0, j * tile:(j + 1) * tile, :], hi)
            o, r_next = _sb_tile(qbd, kt, vt, r_s[...], u, None, hi)
            acc_s[...] += o
            r_s[...] = r_next
    acc = jnp.where(keep, acc_s[...], 0.0)
    out = acc[0:tq]
    for h in range(1, heads):
        out = out + acc[h * tq:(h + 1) * tq]
    o_ref[0] = out.astype(o_ref.dtype)


def _attn_sample(qb, kb, vb, cache_k, cache_v, tile, hi):
    b, tq, sw = qb.shape
    past = cache_k.shape[1]
    per_b = lambda i: (i, 0, 0)
    return pl.pallas_call(
        functools.partial(_attn_sample_kernel, tile=tile, hi=hi),
        grid=(b,),
        in_specs=[pl.BlockSpec((1, tq, sw), per_b), pl.BlockSpec((1, tq, sw), per_b),
                  pl.BlockSpec((1, tq, sw), per_b),
                  pl.BlockSpec((1, past, sw), per_b), pl.BlockSpec((1, past, sw), per_b),
                  _const_spec((tile, tile))],
        out_specs=pl.BlockSpec((1, tq, sw), per_b),
        out_shape=jax.ShapeDtypeStruct((b, tq, sw), F32 if hi else BF16),
        scratch_shapes=[pltpu.VMEM((sw // SB_HD * tq, sw), F32), pltpu.VMEM((sw // SB_HD * tq, 1), F32)],
        compiler_params=_cparams(1, VMEM_MID),
        name="sb_attn_sample",
    )(qb, kb, vb, cache_k, cache_v, _upper_ones(tile))


def _swiglu_rows(act, wg, wu, wd, hi, between=None):
    ff = wg[0].shape[-1]
    n_chunks = ff // FF_CHUNK
    acc = None
    for c in range(n_chunks):
        cs = slice(c * FF_CHUNK, (c + 1) * FF_CHUNK)
        g = _mm(act, wg, cols=cs)
        up = _mm(act, wu, cols=cs)
        part = _mm(_act(_silu(g) * up, hi), wd, rows=cs)
        acc = part if acc is None else acc + part
        if between is not None:
            between(c, n_chunks)
    return acc


def _out_ffn_kernel(x_ref, yl_ref, oa_ref, mod_ref, gmp_ref, gfp_ref, gfo_ref, *refs, hi):
    wo, refs = _take_w(refs, hi)
    wg, refs = _take_w(refs, hi)
    wu, refs = _take_w(refs, hi)
    wd, refs = _take_w(refs, hi)
    out_ref, = refs
    nb, tt, d = x_ref.shape
    lw = yl_ref.shape[2]
    m = mod_ref[...]
    yl = _act(yl_ref[...].reshape(nb * tt, lw), hi)
    oa = _act(oa_ref[...].reshape(nb * tt, oa_ref.shape[2]), hi)
    y = _mm(yl, wo, rows=slice(0, lw)) + _mm(oa, wo, rows=slice(lw, None))
    x1 = x_ref[...] + m[:, 2:3, :] * _rms(y.reshape(nb, tt, d), gmp_ref[...])
    h = _rms(x1, gfp_ref[...]) * (1.0 + m[:, 4:5, :]) + m[:, 3:4, :]
    f = _swiglu_rows(_act(h.reshape(nb * tt, d), hi), wg, wu, wd, hi)
    out_ref[...] = x1 + m[:, 5:6, :] * _rms(f.reshape(nb, tt, d), gfo_ref[...])


def _out_ffn(x, yl, oa, mod, g_mix_post, g_ffn_pre, g_ffn_post, wo, wg, wu, wd, nb, tt, hi):
    b, t, d = x.shape
    row = lambda i, j: (i, j, 0)
    weights = [wo, wg, wu, wd]
    return pl.pallas_call(
        functools.partial(_out_ffn_kernel, hi=hi),
        grid=(b // nb, t // tt),
        in_specs=[pl.BlockSpec((nb, tt, d), row), pl.BlockSpec((nb, tt, yl.shape[2]), row),
                  pl.BlockSpec((nb, tt, oa.shape[2]), row),
                  pl.BlockSpec((nb, 6, d), lambda i, j: (i, 0, 0)),
                  _const_spec((1, d)), _const_spec((1, d)), _const_spec((1, d))]
        + [s for w in weights for s in _w_specs(w, hi)],
        out_specs=pl.BlockSpec((nb, tt, d), row),
        out_shape=jax.ShapeDtypeStruct((b, t, d), F32),
        compiler_params=_cparams(2, VMEM_BIG),
        name="out_proj_ffn",
    )(x, yl, oa, mod, g_mix_post, g_ffn_pre, g_ffn_post, *[a for w in weights for a in _w_args(w, hi)])


def _pool_router_kernel(x_ref, mod_ref, gpre_ref, pbuf_ref, ps_ref, gmp_ref, gfp_ref,
                        rwh_ref, rwl_ref, rb_ref, ltri_ref, *refs, pos0, n_experts, hi):
    pw, refs = _take_w(refs, hi)
    x1_ref, h2_ref, meta_ref, pout_ref, cnt_ref, hp_s, cnt_s = refs
    tt, d = x_ref.shape[1], x_ref.shape[2]
    hal = POOL_BUF + 1
    bi, ti = pl.program_id(0), pl.program_id(1)

    @pl.when((bi == 0) & (ti == 0))
    def _():
        cnt_s[...] = jnp.zeros_like(cnt_s)

    @pl.when(ti == 0)
    def _():
        hp_s[0:1, :] = jnp.zeros((1, d), F32)
        hp_s[1:hal, :] = pbuf_ref[0]

    m = mod_ref[0]
    x = x_ref[0]
    h = _rms(x, gpre_ref[...]) * (1.0 + m[1:2, :]) + m[0:1, :]
    hp_s[hal:hal + tt, :] = h

    pos = pos0 + ti * tt + lax.broadcasted_iota(jnp.int32, (tt, 1), 0)
    gd = d // len(POOL_WINDOWS)
    ys = []
    for g, w in enumerate(POOL_WINDOWS):
        lo = g * gd
        s = hp_s[hal:hal + tt, lo:lo + gd]
        for back in range(1, w):
            s = s + hp_s[hal - back:hal - back + tt, lo:lo + gd]
        cnt = jnp.minimum(w, pos + 1).astype(F32)
        pooled = s / cnt - h[:, lo:lo + gd]
        ys.append(_mm(_act(pooled, hi), (pw[0].at[g], None if pw[1] is None else pw[1].at[g])))
    y = jnp.concatenate(ys, axis=1) * ps_ref[...]
    tail = hp_s[hal + tt - POOL_BUF:hal + tt, :]
    hp_s[1:hal, :] = tail
    pout_ref[0] = tail

    x1 = x + m[2:3, :] * _rms(y, gmp_ref[...])
    x1_ref[0] = x1
    h2 = _rms(x1, gfp_ref[...]) * (1.0 + m[4:5, :]) + m[3:4, :]
    _rows_to_tiles(h2_ref.at[0], h2)

    logits = _mm(_split(h2), (rwh_ref, rwl_ref)) + rb_ref[...]
    lane = lax.broadcasted_iota(jnp.int32, (tt, LANES), 1)
    neg = jnp.float32(-jnp.inf)
    l1 = jnp.where(lane < n_experts, logits, neg)
    m1 = jnp.max(l1, axis=1, keepdims=True)
    i1 = jnp.min(jnp.where(l1 == m1, lane, LANES), axis=1, keepdims=True)
    l2 = jnp.where(lane == i1, neg, l1)
    m2 = jnp.max(l2, axis=1, keepdims=True)
    i2 = jnp.min(jnp.where(l2 == m2, lane, LANES), axis=1, keepdims=True)
    e21 = jnp.exp(m2 - m1)
    g1 = 1.0 / (1.0 + e21)
    g2 = e21 * g1
    sel = ((lane == i1) | (lane == i2)).astype(F32)
    rank = _dot(ltri_ref[...], sel.astype(BF16)) + cnt_s[...]
    r1 = jnp.sum(jnp.where(lane == i1, rank, 0.0), axis=1, keepdims=True)
    r2 = jnp.sum(jnp.where(lane == i2, rank, 0.0), axis=1, keepdims=True)
    cnt_new = cnt_s[...] + jnp.sum(sel, axis=0, keepdims=True)
    cnt_s[...] = cnt_new
    cnt_ref[...] = jnp.broadcast_to(cnt_new, cnt_ref.shape)
    meta = jnp.where(lane == 0, i1.astype(F32), 0.0)
    meta = jnp.where(lane == 1, i2.astype(F32), meta)
    meta = jnp.where(lane == 2, g1, meta)
    meta = jnp.where(lane == 3, g2, meta)
    meta = jnp.where(lane == 4, r1, meta)
    meta = jnp.where(lane == 5, r2, meta)
    meta_ref[0] = meta


def _lower_ones(n):
    i = lax.broadcasted_iota(jnp.int32, (n, n), 0)
    j = lax.broadcasted_iota(jnp.int32, (n, n), 1)
    return (j < i).astype(BF16)


def _pool_router(x, mod, g_pre, pbuf, pool_w, pool_scale, g_mix_post, g_ffn_pre, rw_hi, rw_lo, rb, tt, pos0,
                 n_experts, hi):
    b, t, d = x.shape
    row = lambda i, j: (i, j, 0)
    per_b = lambda i, j: (i, 0, 0)
    return pl.pallas_call(
        functools.partial(_pool_router_kernel, pos0=pos0, n_experts=n_experts, hi=hi),
        grid=(b, t // tt),
        in_specs=[pl.BlockSpec((1, tt, d), row), pl.BlockSpec((1, 6, d), per_b), _const_spec((1, d)),
                  pl.BlockSpec((1, POOL_BUF, d), per_b), _const_spec((1, d)),
                  _const_spec((1, d)), _const_spec((1, d)),
                  _const_spec(rw_hi.shape), _const_spec(rw_lo.shape), _const_spec((1, LANES)),
                  _const_spec((tt, tt))] + _w_specs(pool_w, hi, False),
        out_specs=[pl.BlockSpec((1, tt, d), row),
                   pl.BlockSpec((1, tt * SUBLANES, LANES), row),
                   pl.BlockSpec((1, tt, LANES), row), pl.BlockSpec((1, POOL_BUF, d), per_b),
                   _const_spec((8, LANES))],
        out_shape=[jax.ShapeDtypeStruct((b, t, d), F32), jax.ShapeDtypeStruct((b, t * SUBLANES, LANES), F32),
                   jax.ShapeDtypeStruct((b, t, LANES), F32), jax.ShapeDtypeStruct((b, POOL_BUF, d), F32),
                   jax.ShapeDtypeStruct((8, LANES), F32)],
        scratch_shapes=[pltpu.VMEM((tt + POOL_BUF + 1, d), F32), pltpu.VMEM((1, LANES), F32)],
        compiler_params=_cparams(2, VMEM_MID),
        name="pool_router",
    )(x, mod, g_pre, pbuf, pool_scale, g_mix_post, g_ffn_pre, rw_hi, rw_lo, rb, _lower_ones(tt),
      *_w_args(pool_w, hi))


def _start_row_gather(src_ref, idx_ref, idx_base, idx_stride, dst_ref, sem, n_rows):
    def issue(r, c):
        pltpu.make_async_copy(_tile_of(src_ref, idx_ref[idx_base + r * idx_stride]), _tile_of(dst_ref, r),
                              sem).start()
        return c

    lax.fori_loop(0, n_rows, issue, 0, unroll=8)


def _wait_row_gather(src_ref, dst_ref, sem, n_rows):
    def drain(r, c):
        pltpu.make_async_copy(_tile_of(src_ref, 0), _tile_of(dst_ref, r), sem).wait()
        return c

    lax.fori_loop(0, n_rows, drain, 0, unroll=8)


def _moe_kernel(te_ref, na_ref, tok_ref, h2_ref, wg_ref, wu_ref, wd_ref, ys_ref, xbuf, sem, *, tm):
    i = pl.program_id(0)
    n_active = na_ref[0]
    slot = i % 2

    @pl.when(i == 0)
    def _():
        _start_row_gather(h2_ref, tok_ref, 0, 1, xbuf.at[0], sem.at[0], tm)

    def compute(prefetch):
        def issue_share(c, n):
            for r in range(c * tm // n, (c + 1) * tm // n):
                pltpu.make_async_copy(_tile_of(h2_ref, tok_ref[(i + 1) * tm + r]), _tile_of(xbuf.at[1 - slot], r),
                                      sem.at[1 - slot]).start()

        _wait_row_gather(h2_ref, xbuf.at[slot], sem.at[slot], tm)
        x = _rows_from_tiles(xbuf.at[slot], tm)
        y = _swiglu_rows(_act(x, False), (wg_ref, None), (wu_ref, None), (wd_ref, None), False,
                         between=issue_share if prefetch else None)
        _rows_to_tiles(ys_ref, y)

    @pl.when(i + 1 < n_active)
    def _():
        compute(True)

    @pl.when(i + 1 == n_active)
    def _():
        compute(False)

    @pl.when(i >= n_active)
    def _():
        ys_ref[...] = jnp.zeros_like(ys_ref)


def _moe_grouped(h2_tiles, slot_tok, tile_expert, n_active, wg_b, wu_b, wd_b, tm):
    p = slot_tok.shape[0]
    d, ff = wg_b.shape[1], wg_b.shape[2]
    expert = lambda i, te, na, tok: (te[i], 0, 0)
    return pl.pallas_call(
        functools.partial(_moe_kernel, tm=tm),
        grid_spec=pltpu.PrefetchScalarGridSpec(
            num_scalar_prefetch=3,
            grid=(p // tm,),
            in_specs=[pl.BlockSpec(memory_space=pl.ANY),
                      pl.BlockSpec((None, d, ff), expert), pl.BlockSpec((None, d, ff), expert),
                      pl.BlockSpec((None, ff, d), expert)],
            out_specs=pl.BlockSpec((tm * SUBLANES, LANES), lambda i, te, na, tok: (i, 0)),
            scratch_shapes=[pltpu.VMEM((2, tm * SUBLANES, LANES), F32), pltpu.SemaphoreType.DMA((2,))]),
        out_shape=jax.ShapeDtypeStruct((p * SUBLANES, LANES), F32),
        compiler_params=_cparams(1, VMEM_BIG),
        name="moe_grouped_swiglu",
    )(tile_expert, n_active, slot_tok, h2_tiles, wg_b, wu_b, wd_b)


def _combine_kernel(pos_ref, x1_ref, meta_ref, mod_ref, gfo_ref, ys_ref, out_ref, ybuf, sem):
    nb, tt, d = x1_ref.shape
    rows = nb * tt
    step = pl.program_id(0) * pl.num_programs(1) + pl.program_id(1)
    n_steps = pl.num_programs(0) * pl.num_programs(1)
    slot = step % 2

    @pl.when(step == 0)
    def _():
        for k in range(TOP_K):
            _start_row_gather(ys_ref, pos_ref, k, TOP_K, ybuf.at[0, k], sem.at[0], rows)

    def compute(prefetch):
        for k in range(TOP_K):
            _wait_row_gather(ys_ref, ybuf.at[slot, k], sem.at[slot], rows)
        if prefetch:
            for r in range(rows):
                for k in range(TOP_K):
                    pltpu.make_async_copy(_tile_of(ys_ref, pos_ref[((step + 1) * rows + r) * TOP_K + k]),
                                          _tile_of(ybuf.at[1 - slot, k], r), sem.at[1 - slot]).start()
        m = mod_ref[...]
        meta = meta_ref[...]
        ya = _rows_from_tiles(ybuf.at[slot, 0], rows).reshape(nb, tt, d)
        yb = _rows_from_tiles(ybuf.at[slot, 1], rows).reshape(nb, tt, d)
        y = meta[:, :, 2:3] * ya + meta[:, :, 3:4] * yb
        out_ref[...] = x1_ref[...] + m[:, 5:6, :] * _rms(y, gfo_ref[...])

    @pl.when(step + 1 < n_steps)
    def _():
        compute(True)

    @pl.when(step + 1 == n_steps)
    def _():
        compute(False)


def _combine(x1, ys, pos, meta, mod, g_ffn_post, nb, tt):
    b, t, d = x1.shape
    row = lambda i, j, pos_ref: (i, j, 0)
    return pl.pallas_call(
        _combine_kernel,
        grid_spec=pltpu.PrefetchScalarGridSpec(
            num_scalar_prefetch=1,
            grid=(b // nb, t // tt),
            in_specs=[pl.BlockSpec((nb, tt, d), row), pl.BlockSpec((nb, tt, LANES), row),
                      pl.BlockSpec((nb, 6, d), lambda i, j, pos_ref: (i, 0, 0)),
                      pl.BlockSpec((1, d), lambda i, j, pos_ref: (0, 0)),
                      pl.BlockSpec(memory_space=pl.ANY)],
            out_specs=pl.BlockSpec((nb, tt, d), row),
            scratch_shapes=[pltpu.VMEM((2, TOP_K, nb * tt * SUBLANES, LANES), F32),
                            pltpu.SemaphoreType.DMA((2,))]),
        out_shape=jax.ShapeDtypeStruct((b, t, d), F32),
        compiler_params=_cparams(2, VMEM_MID),
        name="moe_combine",
    )(pos, x1, meta, mod, g_ffn_post, ys)


def _block_diag(w):
    heads, hd, _ = w.shape
    eye = jnp.eye(heads, dtype=w.dtype)
    return jnp.einsum("hij,hg->higj", w, eye).reshape(heads * hd, heads * hd)


def _hi_lo(w):
    hi = w.astype(BF16)
    return hi, (w - hi.astype(F32)).astype(BF16)


def _moe_layer(x1, h2, meta, counts, mod, g_ffn_post, wg_b, wu_b, wd_b, tm, nb, tt):
    b, t, d = x1.shape
    n = b * t
    n_experts = wg_b.shape[0]
    n_tiles = (TOP_K * n) // tm + n_experts
    meta2 = meta.reshape(n, LANES)
    e_idx = meta2[:, 0:2].astype(jnp.int32)
    rank = meta2[:, 4:6].astype(jnp.int32)
    cnt = counts[0, :n_experts].astype(jnp.int32)
    tiles_e = (cnt + tm - 1) // tm
    tiles_end = jnp.cumsum(tiles_e)
    group_off = (tiles_end - tiles_e) * tm
    n_active = tiles_end[-1:]
    tile_id = jnp.minimum(jnp.arange(n_tiles, dtype=jnp.int32), n_active[0] - 1)
    tile_expert = jnp.minimum(jnp.sum((tile_id[:, None] >= tiles_end[None, :]).astype(jnp.int32), axis=1),
                              n_experts - 1)
    pos = group_off[e_idx] + rank
    tok = jnp.broadcast_to(jnp.arange(n, dtype=jnp.int32)[:, None], (n, TOP_K))
    slot_tok = jnp.zeros((n_tiles * tm,), jnp.int32).at[pos.reshape(-1)].set(tok.reshape(-1))

    ys = _moe_grouped(h2.reshape(n * SUBLANES, LANES), slot_tok, tile_expert, n_active.astype(jnp.int32),
                      wg_b, wu_b, wd_b, tm)
    return _combine(x1, ys, pos.reshape(-1), meta, mod, g_ffn_post, nb, tt)


def _trunk(x, mod_all, pos0, cache_k, cache_v, st_h, st_conv, st_pool, p, cfg):
    b, t, d = x.shape
    depth = mod_all.shape[0]
    nb, tt = cfg["nb"], cfg["tt"]
    lw = p["conv_w"].shape[2]
    sw = (p["w_in"][0][0].shape[1] - 2 * lw) // 3
    heads = sw // SB_HD
    n_experts = p["moe_wg_b"].shape[1]
    row1 = lambda a: a.reshape(1, -1)
    new_k, new_v, new_h, new_conv, new_pool = [], [], [], [], []
    for l in range(depth):
        e = l // 2
        hi = l in cfg["hi_layers"]
        mod = mod_all[l].reshape(b, 6, d)
        if l % 2 == 0:
            proj = _in_proj(x, mod, row1(p["g_mix_pre"][l]), p["w_in"][e], nb, tt, lw, sw, hi)
            gate, lru_in, q, k32, v32 = proj[:5]
            if cache_k is None:
                cbuf = jnp.zeros((b, CONV_W - 1, lw), F32)
                h0 = jnp.zeros((b, lw), F32)
            else:
                cbuf, h0 = st_conv[e], st_h[e]
            yl, h_last, cb = _lru(lru_in, gate, cbuf, h0, p["conv_w"][e], p["conv_b"][e], p["wa"][e],
                                  p["lru_ba"][e], p["wx"][e], p["lru_bx"][e], p["lru_lambda"][e], cfg["lru_tt"], hi)
            kk, vv = (k32, v32) if hi else proj[5:7]
            if cache_k is None:
                oa = _attn_prompt(q, kk, vv, cfg["attn_tile"], hi)
            else:
                past = cache_k.shape[2]
                oa = _attn_sample(q, kk, vv, cache_k[e].reshape(b, past, sw),
                                  cache_v[e].reshape(b, past, sw), cfg["attn_tile"], hi)
            x = _out_ffn(x, yl, oa, mod, row1(p["g_mix_post"][l]), row1(p["g_ffn_pre"][l]),
                         row1(p["g_ffn_post"][l]), p["w_out"][e], p["ffn_wg"][e], p["ffn_wu"][e],
                         p["ffn_wd"][e], nb, cfg["hi_tt"] if hi else tt, hi)
            new_k.append(k32.reshape(b, t, heads, SB_HD))
            new_v.append(v32.reshape(b, t, heads, SB_HD))
            new_h.append(h_last.reshape(b, lw))
            new_conv.append(cb)
        else:
            pbuf = jnp.zeros((b, POOL_BUF, d), F32) if st_pool is None else st_pool[e]
            x1, h2, meta, pb, counts = _pool_router(
                x, mod, row1(p["g_mix_pre"][l]), pbuf, p["pool_w"][e], row1(p["pool_scale"][e]),
                row1(p["g_mix_post"][l]), row1(p["g_ffn_pre"][l]), p["rw_hi"][e], p["rw_lo"][e], p["rb"][e],
                cfg["pool_tt"], pos0, n_experts, hi)
            x = _moe_layer(x1, h2, meta, counts, mod, row1(p["g_ffn_post"][l]), p["moe_wg_b"][e], p["moe_wu_b"][e],
                           p["moe_wd_b"][e], cfg["moe_tm"], nb, tt)
            new_pool.append(pb)
    return (x, jnp.stack(new_k), jnp.stack(new_v), jnp.stack(new_h), jnp.stack(new_conv), jnp.stack(new_pool))


def kernel(x_prompt, x_sample, cache_sb_k, cache_sb_v, state_lru_h, state_lru_conv, state_pool, c_prompt, c_sample, ada_w, ada_b, g_mix_pre, g_mix_post, g_ffn_pre, g_ffn_post, w_in, w_out, conv_w, conv_b, lru_wa, lru_ba, lru_wx, lru_bx, lru_lambda, pool_w, pool_scale, ffn_w_gate, ffn_w_up, ffn_w_down, router_w, router_b, moe_w_gate, moe_w_up, moe_w_down):
    bp = x_prompt.shape[0]
    n_experts = router_w.shape[2]
    rw_hi, rw_lo = _hi_lo(jnp.pad(router_w, ((0, 0), (0, 0), (0, LANES - n_experts))))

    def per_layer(w):
        return [_hi_lo(w[e]) if e == 0 else (w[e].astype(BF16), None) for e in range(w.shape[0])]

    p = dict(
        g_mix_pre=g_mix_pre, g_mix_post=g_mix_post, g_ffn_pre=g_ffn_pre, g_ffn_post=g_ffn_post,
        w_in=per_layer(w_in), w_out=per_layer(w_out),
        conv_w=conv_w, conv_b=conv_b, lru_ba=lru_ba, lru_bx=lru_bx, lru_lambda=lru_lambda,
        wa=per_layer(jax.vmap(_block_diag)(lru_wa)), wx=per_layer(jax.vmap(_block_diag)(lru_wx)),
        pool_w=per_layer(pool_w), pool_scale=pool_scale,
        ffn_wg=per_layer(ffn_w_gate), ffn_wu=per_layer(ffn_w_up), ffn_wd=per_layer(ffn_w_down),
        rw_hi=rw_hi, rw_lo=rw_lo,
        rb=jnp.pad(router_b, ((0, 0), (0, LANES - n_experts))).reshape(-1, 1, LANES),
        moe_wg_b=moe_w_gate.astype(BF16), moe_wu_b=moe_w_up.astype(BF16), moe_wd_b=moe_w_down.astype(BF16),
    )
    mod_all = _ada_mod(jnp.concatenate([c_prompt, c_sample], axis=0), ada_w, ada_b)
    cfg_prompt = dict(nb=1, tt=512, hi_tt=256, lru_tt=512, pool_tt=512, attn_tile=256, moe_tm=512, hi_layers=(0, 1))
    ts = x_sample.shape[1]
    cfg_sample = dict(nb=x_sample.shape[0] // 2, tt=ts, hi_tt=ts, lru_tt=ts, pool_tt=ts, attn_tile=256, moe_tm=128,
                      hi_layers=(0, 1))
    yp, kp, vp, hp, cp, pp = _trunk(x_prompt, mod_all[:, :bp], 0, None, None, None, None, None, p, cfg_prompt)
    ys, ks, vs, hs, cs, ps = _trunk(x_sample, mod_all[:, bp:], cache_sb_k.shape[2], cache_sb_k, cache_sb_v,
                                    state_lru_h, state_lru_conv, state_pool, p, cfg_sample)
    return (yp, ys, kp, vp, hp, cp, pp, ks, vs, hs, cs, ps)
```
